```python
import jax, jax.numpy as jnp
from jax import lax
import numpy as np

D_MODEL = 1024
BATCH = 4
SEQ = 4096
DEPTH = 2
DEC_BATCH = 8
DEC_SEQ = 8192
PAST_LEN = 128

N_MIXERS = 2
RET_HEADS = 4
RET_DK = D_MODEL // RET_HEADS
RET_DV = 2 * RET_DK
RET_QK_WIDTH = RET_HEADS * RET_DK
RET_V_WIDTH = RET_HEADS * RET_DV
RET_CHUNK = 128
RET_DECAY_FWD = 5.0
RET_DECAY_BWD = 5.5
ROPE_BASE = 10000.0
HGRN_EXPAND = 128
HGRN_HEADS = D_MODEL // HGRN_EXPAND
HGRN_DK = HGRN_EXPAND
HGRN_DV = D_MODEL // HGRN_HEADS
HGRN_F_WIDTH = HGRN_HEADS * HGRN_DK
HGRN_CHUNK = 64
N_EXPERTS = 16
EC_CAPACITY_FACTOR = 2
EXPERT_FF = 2 * D_MODEL
NORM_EPS = 1e-6

kernel_name = "hybrid_retention_hgrn2_ec_moe_encoder"


def _rms_norm(x, w):
    xf = x.astype(jnp.float32)
    y = xf * lax.rsqrt(jnp.mean(xf * xf, axis=-1, keepdims=True) + NORM_EPS)
    return (y * w.astype(jnp.float32)).astype(x.dtype)


def _rms_plain(xf):
    return xf * lax.rsqrt(jnp.mean(xf * xf, axis=-1, keepdims=True) + NORM_EPS)


def _flip(a):
    return jnp.flip(a, axis=1)


def _rope(x):
    T, d = x.shape[1], x.shape[-1]
    half = d // 2
    inv = 1.0 / (ROPE_BASE ** (jnp.arange(half, dtype=jnp.float32) / half))
    ang = jnp.arange(T, dtype=jnp.float32)[:, None] * inv[None, :]
    cos = jnp.cos(ang)[None, :, None, :]
    sin = jnp.sin(ang)[None, :, None, :]
    x1, x2 = x[..., :half], x[..., half:]
    return jnp.concatenate([x1 * cos - x2 * sin, x1 * sin + x2 * cos], axis=-1)


def _ret_log_decay(offset):
    return jnp.log1p(-jnp.exp2(-offset - jnp.arange(RET_HEADS, dtype=jnp.float32)))


def _retention_dir(q, k, v, log_gamma, strict):
    B, T, H, DK = q.shape
    DV = v.shape[-1]
    C = RET_CHUNK
    N = T // C

    def to_chunks(a):
        return jnp.moveaxis(a.reshape(B, N, C, H, a.shape[-1]), 1, 0)

    pos = jnp.arange(C, dtype=jnp.float32)
    rel = pos[:, None] - pos[None, :]
    mask = (rel > 0) if strict else (rel >= 0)
    intra = jnp.where(mask[None], jnp.exp(jnp.where(mask, rel, 0.0)[None] * log_gamma[:, None, None]), 0.0)
    q_dec = jnp.exp((pos + 1.0)[:, None] * log_gamma[None, :])[None, :, :, None]
    k_dec = jnp.exp((C - 1.0 - pos)[:, None] * log_gamma[None, :])[None, :, :, None]
    chunk_dec = jnp.exp(C * log_gamma)[None, :, None, None]

    def step(R, blk):
        qb, kb, vb = blk
        s = jnp.einsum('bchd,bshd->bhcs', qb, kb) * intra[None]
        o = jnp.einsum('bhcs,bshv->bchv', s, vb) + jnp.einsum('bchd,bhdv->bchv', qb * q_dec, R)
        R = R * chunk_dec + jnp.einsum('bshd,bshv->bhdv', kb * k_dec, vb)
        return R, o

    R0 = jnp.zeros((B, H, DK, DV), jnp.float32)
    _, o = lax.scan(step, R0, (to_chunks(q), to_chunks(k), to_chunks(v)))
    return jnp.moveaxis(o, 0, 1).reshape(B, T, H, DV)


def _retention_mixer(x, w_in, w_out):
    B, T, _ = x.shape
    proj = x @ w_in
    q, k, v, g = jnp.split(proj, [RET_QK_WIDTH, 2 * RET_QK_WIDTH, 2 * RET_QK_WIDTH + RET_V_WIDTH], axis=-1)
    q = _rope(q.astype(jnp.float32).reshape(B, T, RET_HEADS, RET_DK))
    k = _rope(k.astype(jnp.float32).reshape(B, T, RET_HEADS, RET_DK)) * (RET_DK ** -0.5)
    v = v.astype(jnp.float32).reshape(B, T, RET_HEADS, RET_DV)
    o_fwd = _retention_dir(q, k, v, _ret_log_decay(RET_DECAY_FWD), strict=False)
    o_bwd = _flip(_retention_dir(_flip(q), _flip(k), _flip(v), _ret_log_decay(RET_DECAY_BWD), strict=True))
    o = _rms_plain(o_fwd + o_bwd).reshape(B, T, RET_V_WIDTH)
    o = o.astype(x.dtype) * jax.nn.silu(g)
    return o @ w_out


def _hgrn2_dir(q, k, v, log_f):
    B, T, H, DK = q.shape
    DV = v.shape[-1]
    C = HGRN_CHUNK
    N = T // C

    def to_chunks(a):
        return jnp.moveaxis(a.reshape(B, N, C, H, a.shape[-1]), 1, 0)

    causal = jnp.tril(jnp.ones((C, C), dtype=bool))

    def step(S, blk):
        qb, kb, vb, gb = blk
        G = jnp.cumsum(gb, axis=1)
        G_last = G[:, -1:]
        qg = qb * jnp.exp(G)
        s = jnp.einsum('bchd,bshd->bhcs', qg, kb * jnp.exp(-G))
        s = jnp.where(causal, s, 0.0)
        o = jnp.einsum('bhcs,bshv->bchv', s, vb) + jnp.einsum('bchd,bhdv->bchv', qg, S)
        S = S * jnp.exp(G_last[:, 0])[..., None] + jnp.einsum('bshd,bshv->bhdv', kb * jnp.exp(G_last - G), vb)
        return S, o

    S0 = jnp.zeros((B, H, DK, DV), jnp.float32)
    _, o = lax.scan(step, S0, (to_chunks(q), to_chunks(k), to_chunks(v), to_chunks(log_f)))
    return jnp.moveaxis(o, 0, 1).reshape(B, T, H, DV)


def _hgrn2_mixer(x, w_in, lower_bound, norm_w, w_out):
    B, T, _ = x.shape
    proj = x @ w_in
    q, i, f_fwd, f_bwd, g = jnp.split(
        proj, [HGRN_F_WIDTH, HGRN_F_WIDTH + D_MODEL, 2 * HGRN_F_WIDTH + D_MODEL, 3 * HGRN_F_WIDTH + D_MODEL], axis=-1)
    q = jax.nn.silu(q.astype(jnp.float32)).reshape(B, T, HGRN_HEADS, HGRN_DK)
    i = i.astype(jnp.float32).reshape(B, T, HGRN_HEADS, HGRN_DV)

    def gates(z, lb):
        f = lb + (1.0 - lb) * jax.nn.sigmoid(z.astype(jnp.float32))
        return (1.0 - f).reshape(B, T, HGRN_HEADS, HGRN_DK), jnp.log(f).reshape(B, T, HGRN_HEADS, HGRN_DK)

    k_f, lf_f = gates(f_fwd, lower_bound[0])
    k_b, lf_b = gates(f_bwd, lower_bound[1])
    o_fwd = _hgrn2_dir(q, k_f, i, lf_f)
    o_bwd = _flip(_hgrn2_dir(_flip(q), _flip(k_b), _flip(i), _flip(lf_b)))
    o = _rms_norm((o_fwd + o_bwd).reshape(B, T, D_MODEL), norm_w).astype(x.dtype)
    o = o * jax.nn.sigmoid(g)
    return o @ w_out


def _expert_choice_ffn(x, w_router, w_gate, w_up, w_down):
    B, T, D = x.shape
    n_tok = B * T
    cap = EC_CAPACITY_FACTOR * n_tok // N_EXPERTS
    xt = x.reshape(n_tok, D)
    affinity = jax.nn.softmax((xt @ w_router).astype(jnp.float32), axis=-1)
    gate_vals, tok_idx = lax.top_k(affinity.T, cap)
    xe = xt[tok_idx]
    h = jax.nn.silu(jnp.einsum('ecd,edf->ecf', xe, w_gate)) * jnp.einsum('ecd,edf->ecf', xe, w_up)
    ye = jnp.einsum('ecf,efd->ecd', h, w_down) * gate_vals[..., None].astype(x.dtype)
    y = jnp.zeros_like(xt).at[tok_idx.reshape(-1)].add(ye.reshape(-1, D))
    return y.reshape(B, T, D)


def _trunk(x, norm_mix_w, norm_ffn_w, norm_final_w, ret_w_in, ret_w_out, hgrn_w_in, hgrn_lb,
           hgrn_norm_w, hgrn_w_out, moe_w_router, moe_w_gate, moe_w_up, moe_w_down):
    p = jax.nn.softmax(hgrn_lb.astype(jnp.float32), axis=0)
    lower_bounds = jnp.cumsum(p, axis=0) - p[0:1]
    for layer in range(DEPTH):
        h = _rms_norm(x, norm_mix_w[layer])
        j = layer // N_MIXERS
        if layer % N_MIXERS == 0:
            x = x + _retention_mixer(h, ret_w_in[j], ret_w_out[j])
        else:
            x = x + _hgrn2_mixer(h, hgrn_w_in[j], lower_bounds[layer], hgrn_norm_w[j], hgrn_w_out[j])
        x = x + _expert_choice_ffn(_rms_norm(x, norm_ffn_w[layer]), moe_w_router[layer],
                                   moe_w_gate[layer], moe_w_up[layer], moe_w_down[layer])
    return _rms_norm(x, norm_final_w)


def setup_inputs(seed: int = 0) -> dict:
    key = jax.random.key(seed)
    ks = jax.random.split(key, 16)
    n_ret = (DEPTH + 1) // 2
    n_hgrn = DEPTH // 2
    ret_in_width = 2 * RET_QK_WIDTH + 2 * RET_V_WIDTH
    hgrn_in_width = 3 * HGRN_F_WIDTH + 2 * D_MODEL

    def nrm(k, shape, fan_in):
        return jax.random.normal(k, shape, jnp.float32) * (fan_in ** -0.5)

    def gain(k, shape):
        return 1.0 + 0.05 * jax.random.normal(k, shape, jnp.float32)

    return {
        "x_prompt": jax.random.normal(ks[0], (BATCH, SEQ, D_MODEL), jnp.float32),
        "x_sample": jax.random.normal(ks[1], (DEC_BATCH, DEC_SEQ, D_MODEL), jnp.float32),
        "norm_mix_w": gain(ks[2], (DEPTH, D_MODEL)),
        "norm_ffn_w": gain(ks[3], (DEPTH, D_MODEL)),
        "norm_final_w": gain(ks[4], (D_MODEL,)),
        "ret_w_in": nrm(ks[5], (n_ret, D_MODEL, ret_in_width), D_MODEL),
        "ret_w_out": nrm(ks[6], (n_ret, RET_V_WIDTH, D_MODEL), RET_V_WIDTH),
        "hgrn_w_in": nrm(ks[7], (n_hgrn, D_MODEL, hgrn_in_width), D_MODEL),
        "hgrn_lb": 0.1 * jax.random.normal(ks[8], (DEPTH, 2, HGRN_F_WIDTH), jnp.float32),
        "hgrn_norm_w": gain(ks[9], (n_hgrn, D_MODEL)),
        "hgrn_w_out": nrm(ks[10], (n_hgrn, D_MODEL, D_MODEL), D_MODEL),
        "moe_w_router": nrm(ks[11], (DEPTH, D_MODEL, N_EXPERTS), D_MODEL),
        "moe_w_gate": nrm(ks[12], (DEPTH, N_EXPERTS, D_MODEL, EXPERT_FF), D_MODEL),
        "moe_w_up": nrm(ks[13], (DEPTH, N_EXPERTS, D_MODEL, EXPERT_FF), D_MODEL),
        "moe_w_down": nrm(ks[14], (DEPTH, N_EXPERTS, EXPERT_FF, D_MODEL), EXPERT_FF),
    }


def reference(x_prompt, x_sample, norm_mix_w, norm_ffn_w, norm_final_w, ret_w_in, ret_w_out,
              hgrn_w_in, hgrn_lb, hgrn_norm_w, hgrn_w_out, moe_w_router, moe_w_gate, moe_w_up, moe_w_down):
    y_prompt = _trunk(x_prompt, norm_mix_w, norm_ffn_w, norm_final_w, ret_w_in, ret_w_out, hgrn_w_in,
                      hgrn_lb, hgrn_norm_w, hgrn_w_out, moe_w_router, moe_w_gate, moe_w_up, moe_w_down)
    y_sample = _trunk(x_sample, norm_mix_w, norm_ffn_w, norm_final_w, ret_w_in, ret_w_out, hgrn_w_in,
                      hgrn_lb, hgrn_norm_w, hgrn_w_out, moe_w_router, moe_w_gate, moe_w_up, moe_w_down)
    return (y_prompt, y_sample)
```

```python
import functools
import math

import jax
import jax.numpy as jnp
from jax import lax
from jax.experimental import pallas as pl
from jax.experimental.pallas import tpu as pltpu

F32 = jnp.float32
BF16 = jnp.bfloat16

D_MODEL = 1024
NORM_EPS = 1e-6
ROPE_BASE = 10000.0

RET_HEADS = 4
RET_DK = 256
RET_DV = 512
RET_QK_WIDTH = RET_HEADS * RET_DK
RET_V_WIDTH = RET_HEADS * RET_DV
RET_DECAY_FWD = 5.0
RET_DECAY_BWD = 5.5
RET_BLOCK = 256

HGRN_HEADS = 8
HGRN_DK = 128
HGRN_CHUNK = 64
HGRN_BLOCK = 256

N_EXPERTS = 16
EC_CAPACITY_FACTOR = 2
EXPERT_FF = 2 * D_MODEL

PROJ_TM = 512
PROJ_CW = 512
FFN_TM = 512
FFN_FW = 512
LANE = 128

VMEM_LIMIT_BYTES = 52 * 1024 * 1024


def _cparams(sem):
    return pltpu.CompilerParams(dimension_semantics=sem, vmem_limit_bytes=VMEM_LIMIT_BYTES)


def _ret_log_gamma(offset):
    return [math.log1p(-(2.0 ** (-offset - h))) for h in range(RET_HEADS)]


def _rms_rows(x, w):
    ms = jnp.mean(x * x, axis=-1, keepdims=True)
    return x * lax.rsqrt(ms + NORM_EPS) * w


def _ret_inproj_kernel(x_ref, nw_ref, w_ref, cos_ref, sin_ref, o_ref, xn_scr):
    xn_scr[...] = _rms_rows(x_ref[...], nw_ref[...]).astype(BF16)
    cos = cos_ref[...]
    sin = sin_ref[...]
    width = w_ref.shape[1]
    half = RET_DK // 2
    for j in range(width // PROJ_CW):
        c0 = j * PROJ_CW
        r = jnp.dot(xn_scr[...], w_ref[:, c0:c0 + PROJ_CW], preferred_element_type=F32)
        if c0 < 2 * RET_QK_WIDTH:
            scale = 1.0 if c0 < RET_QK_WIDTH else RET_DK ** -0.5
            parts = []
            for hh in range(PROJ_CW // RET_DK):
                x1 = r[:, hh * RET_DK:hh * RET_DK + half]
                x2 = r[:, hh * RET_DK + half:(hh + 1) * RET_DK]
                parts.append((x1 * cos - x2 * sin) * scale)
                parts.append((x1 * sin + x2 * cos) * scale)
            r = jnp.concatenate(parts, axis=-1)
        elif c0 >= 2 * RET_QK_WIDTH + RET_V_WIDTH:
            r = r * jax.nn.sigmoid(r)
        o_ref[:, c0:c0 + PROJ_CW] = r.astype(BF16)


def _ret_inproj(x2d, col, nw, w, cos, sin, seq):
    n = x2d.shape[0]
    width = w.shape[1]
    tpb = seq // PROJ_TM
    return pl.pallas_call(
        _ret_inproj_kernel,
        grid=(n // PROJ_TM,),
        in_specs=[
            pl.BlockSpec((PROJ_TM, D_MODEL), lambda i: (i, col)),
            pl.BlockSpec((1, D_MODEL), lambda i: (0, 0)),
            pl.BlockSpec((D_MODEL, width), lambda i: (0, 0)),
            pl.BlockSpec((PROJ_TM, RET_DK // 2), lambda i: (i % tpb, 0)),
            pl.BlockSpec((PROJ_TM, RET_DK // 2), lambda i: (i % tpb, 0)),
        ],
        out_specs=pl.BlockSpec((PROJ_TM, width), lambda i: (i, 0)),
        out_shape=jax.ShapeDtypeStruct((n, width), BF16),
        scratch_shapes=[pltpu.VMEM((PROJ_TM, D_MODEL), BF16)],
        compiler_params=_cparams(("arbitrary",)),
        name="ret_inproj",
    )(x2d, nw, w, cos, sin)


def _hgrn_inproj_kernel(x_ref, nw_ref, w_ref, lb_ref, a_ref, l_ref, xn_scr):
    xn_scr[...] = _rms_rows(x_ref[...], nw_ref[...]).astype(BF16)
    width = w_ref.shape[1]
    for j in range(width // PROJ_CW):
        c0 = j * PROJ_CW
        r = jnp.dot(xn_scr[...], w_ref[:, c0:c0 + PROJ_CW], preferred_element_type=F32)
        if c0 < D_MODEL:
            a_ref[:, c0:c0 + PROJ_CW] = (r * jax.nn.sigmoid(r)).astype(BF16)
        elif c0 < 2 * D_MODEL:
            a_ref[:, c0:c0 + PROJ_CW] = r.astype(BF16)
        elif c0 < 4 * D_MODEL:
            g0 = c0 - 2 * D_MODEL
            lb = lb_ref[:, g0:g0 + PROJ_CW]
            f = lb + (1.0 - lb) * jax.nn.sigmoid(r)
            a_ref[:, c0:c0 + PROJ_CW] = (1.0 - f).astype(BF16)
            l_ref[:, g0:g0 + PROJ_CW] = jnp.log(f)
        else:
            a_ref[:, c0:c0 + PROJ_CW] = jax.nn.sigmoid(r).astype(BF16)


def _hgrn_inproj(xy, nw, w, lb):
    n = xy.shape[0]
    width = w.shape[1]
    return pl.pallas_call(
        _hgrn_inproj_kernel,
        grid=(n // PROJ_TM,),
        in_specs=[
            pl.BlockSpec((PROJ_TM, D_MODEL), lambda i: (i, 1)),
            pl.BlockSpec((1, D_MODEL), lambda i: (0, 0)),
            pl.BlockSpec((D_MODEL, width), lambda i: (0, 0)),
            pl.BlockSpec((1, 2 * D_MODEL), lambda i: (0, 0)),
        ],
        out_specs=[
            pl.BlockSpec((PROJ_TM, width), lambda i: (i, 0)),
            pl.BlockSpec((PROJ_TM, 2 * D_MODEL), lambda i: (i, 0)),
        ],
        out_shape=[
            jax.ShapeDtypeStruct((n, width), BF16),
            jax.ShapeDtypeStruct((n, 2 * D_MODEL), F32),
        ],
        scratch_shapes=[pltpu.VMEM((PROJ_TM, D_MODEL), BF16)],
        compiler_params=_cparams(("arbitrary",)),
        name="hgrn_inproj",
    )(xy, nw, w, lb)


_NT = (((1,), (1,)), ((), ()))
_TN = (((0,), (0,)), ((), ()))


def _ret_bwd_kernel(q_ref, k_ref, v_ref, ob_ref, r_scr, *, lg):
    @pl.when(pl.program_id(1) == 0)
    def _():
        r_scr[...] = jnp.zeros_like(r_scr)

    c = q_ref.shape[0]
    pos = lax.broadcasted_iota(jnp.int32, (c, 1), 0).astype(F32)
    for h in range(RET_HEADS):
        qh = q_ref[:, h * RET_DK:(h + 1) * RET_DK].astype(F32)
        kh = k_ref[:, h * RET_DK:(h + 1) * RET_DK].astype(F32)
        vh = v_ref[:, h * RET_DV:(h + 1) * RET_DV]
        qd = (qh * jnp.exp((c - pos) * lg[h])).astype(BF16)
        o = jnp.dot(qd, r_scr[h].astype(BF16), preferred_element_type=F32)
        ob_ref[:, h * RET_DV:(h + 1) * RET_DV] = o.astype(BF16)
        kd = (kh * jnp.exp(pos * lg[h])).astype(BF16)
        r_scr[h] = r_scr[h] * math.exp(c * lg[h]) + lax.dot_general(
            kd, vh, _TN, preferred_element_type=F32)


def _ret_fwd_kernel(q_ref, k_ref, v_ref, g_ref, ob_ref, x_ref, m_ref, wo_ref, xy_ref,
                    r_scr, *, lg):
    @pl.when(pl.program_id(1) == 0)
    def _():
        r_scr[...] = jnp.zeros_like(r_scr)

    c = q_ref.shape[0]
    pos = lax.broadcasted_iota(jnp.int32, (c, 1), 0).astype(F32)
    acc = x_ref[...]
    for h in range(RET_HEADS):
        qb = q_ref[:, h * RET_DK:(h + 1) * RET_DK]
        kb = k_ref[:, h * RET_DK:(h + 1) * RET_DK]
        vh = v_ref[:, h * RET_DV:(h + 1) * RET_DV]
        s = lax.dot_general(qb, kb, _NT, preferred_element_type=F32)
        sm = (s * m_ref[h]).astype(BF16)
        qd = (qb.astype(F32) * jnp.exp((pos + 1.0) * lg[h])).astype(BF16)
        o = (jnp.dot(sm, vh, preferred_element_type=F32)
             + jnp.dot(qd, r_scr[h].astype(BF16), preferred_element_type=F32)
             + ob_ref[:, h * RET_DV:(h + 1) * RET_DV].astype(F32))
        kd = (kb.astype(F32) * jnp.exp((c - 1.0 - pos) * lg[h])).astype(BF16)
        r_scr[h] = r_scr[h] * math.exp(c * lg[h]) + lax.dot_general(
            kd, vh, _TN, preferred_element_type=F32)
        ms = jnp.mean(o * o, axis=-1, keepdims=True)
        gated = (o * lax.rsqrt(ms + NORM_EPS)
                 * g_ref[:, h * RET_DV:(h + 1) * RET_DV].astype(F32)).astype(BF16)
        acc = acc + jnp.dot(gated, wo_ref[h * RET_DV:(h + 1) * RET_DV, :],
                            preferred_element_type=F32)
    xy_ref[:, :D_MODEL] = acc
    xy_ref[:, D_MODEL:] = acc


def _retention(proj, x3, decay_mat, w_out):
    b, t, _ = proj.shape
    c = RET_BLOCK
    nc = t // c
    lgf = _ret_log_gamma(RET_DECAY_FWD)
    lgb = _ret_log_gamma(RET_DECAY_BWD)
    qk = RET_QK_WIDTH
    vw = RET_V_WIDTH
    state = pltpu.VMEM((RET_HEADS, RET_DK, RET_DV), F32)

    ob = pl.pallas_call(
        functools.partial(_ret_bwd_kernel, lg=lgb),
        grid=(b, nc),
        in_specs=[
            pl.BlockSpec((None, c, qk), lambda bi, i: (bi, nc - 1 - i, 0)),
            pl.BlockSpec((None, c, qk), lambda bi, i: (bi, nc - 1 - i, 1)),
            pl.BlockSpec((None, c, vw), lambda bi, i: (bi, nc - 1 - i, 1)),
        ],
        out_specs=pl.BlockSpec((None, c, vw), lambda bi, i: (bi, nc - 1 - i, 0)),
        out_shape=jax.ShapeDtypeStruct((b, t, vw), BF16),
        scratch_shapes=[state],
        compiler_params=_cparams(("arbitrary", "arbitrary")),
        name="ret_bwd",
    )(proj, proj, proj)

    return pl.pallas_call(
        functools.partial(_ret_fwd_kernel, lg=lgf),
        grid=(b, nc),
        in_specs=[
            pl.BlockSpec((None, c, qk), lambda bi, i: (bi, i, 0)),
            pl.BlockSpec((None, c, qk), lambda bi, i: (bi, i, 1)),
            pl.BlockSpec((None, c, vw), lambda bi, i: (bi, i, 1)),
            pl.BlockSpec((None, c, vw), lambda bi, i: (bi, i, 2)),
            pl.BlockSpec((None, c, vw), lambda bi, i: (bi, i, 0)),
            pl.BlockSpec((None, c, D_MODEL), lambda bi, i: (bi, i, 0)),
            pl.BlockSpec((RET_HEADS, c, c), lambda bi, i: (0, 0, 0)),
            pl.BlockSpec((vw, D_MODEL), lambda bi, i: (0, 0)),
        ],
        out_specs=pl.BlockSpec((None, c, 2 * D_MODEL), lambda bi, i: (bi, i, 0)),
        out_shape=jax.ShapeDtypeStruct((b, t, 2 * D_MODEL), F32),
        scratch_shapes=[state],
        compiler_params=_cparams(("arbitrary", "arbitrary")),
        name="ret_fwd",
    )(proj, proj, proj, proj, ob, x3, decay_mat, w_out)


def _ret_decay_matrix():
    c = RET_BLOCK
    rel = (jnp.arange(c, dtype=F32)[:, None] - jnp.arange(c, dtype=F32)[None, :])[None]
    lgf = jnp.asarray(_ret_log_gamma(RET_DECAY_FWD), F32)[:, None, None]
    lgb = jnp.asarray(_ret_log_gamma(RET_DECAY_BWD), F32)[:, None, None]
    return jnp.where(rel >= 0, jnp.exp(jnp.maximum(rel, 0.0) * lgf),
                     jnp.exp(jnp.maximum(-rel, 0.0) * lgb))


def _tri_cumsum(tri, x):
    hi = x.astype(BF16)
    r1 = x - hi.astype(F32)
    mid = r1.astype(BF16)
    lo = (r1 - mid.astype(F32)).astype(BF16)
    return (jnp.dot(tri, hi, preferred_element_type=F32)
            + jnp.dot(tri, mid, preferred_element_type=F32)
            + jnp.dot(tri, lo, preferred_element_type=F32))


def _hgrn_chunk(q, k, v, lf, s_scr, reverse):
    c = q.shape[0]
    row = lax.broadcasted_iota(jnp.int32, (c, c), 0)
    col = lax.broadcasted_iota(jnp.int32, (c, c), 1)
    keep = (col >= row) if reverse else (col <= row)
    g = _tri_cumsum(keep.astype(BF16), lf)
    g_last = g[0:1, :] if reverse else g[c - 1:c, :]
    qg = (q.astype(F32) * jnp.exp(g)).astype(BF16)
    kf = k.astype(F32)
    kg = (kf * jnp.exp(-g)).astype(BF16)
    kd = (kf * jnp.exp(g_last - g)).astype(BF16)
    e_last = jnp.exp(g_last)
    outs = []
    for h in range(HGRN_HEADS):
        cs = slice(h * HGRN_DK, (h + 1) * HGRN_DK)
        s = lax.dot_general(qg[:, cs], kg[:, cs], _NT, preferred_element_type=F32)
        s = jnp.where(keep, s, 0.0).astype(BF16)
        st = s_scr[h]
        o = (jnp.dot(s, v[:, cs], preferred_element_type=F32)
             + lax.dot_general(qg[:, cs], st.astype(BF16), _NT, preferred_element_type=F32))
        s_scr[h] = st * e_last[:, cs] + lax.dot_general(
            v[:, cs], kd[:, cs], _TN, preferred_element_type=F32)
        outs.append(o)
    return jnp.concatenate(outs, axis=-1)


def _hgrn_bwd_kernel(q_ref, v_ref, k_ref, lf_ref, ob_ref, s_scr):
    @pl.when(pl.program_id(1) == 0)
    def _():
        s_scr[...] = jnp.zeros_like(s_scr)

    for sc in reversed(range(HGRN_BLOCK // HGRN_CHUNK)):
        rows = slice(sc * HGRN_CHUNK, (sc + 1) * HGRN_CHUNK)
        o = _hgrn_chunk(q_ref[rows, :], k_ref[rows, :], v_ref[rows, :], lf_ref[rows, :],
                        s_scr, reverse=True)
        ob_ref[rows, :] = o.astype(BF16)


def _hgrn_fwd_kernel(q_ref, v_ref, k_ref, lf_ref, sg_ref, ob_ref, x_ref, nw_ref, wo_ref,
                     xy_ref, s_scr, o_scr):
    @pl.when(pl.program_id(1) == 0)
    def _():
        s_scr[...] = jnp.zeros_like(s_scr)

    for sc in range(HGRN_BLOCK // HGRN_CHUNK):
        rows = slice(sc * HGRN_CHUNK, (sc + 1) * HGRN_CHUNK)
        o = _hgrn_chunk(q_ref[rows, :], k_ref[rows, :], v_ref[rows, :], lf_ref[rows, :],
                        s_scr, reverse=False)
        o_scr[rows, :] = o + ob_ref[rows, :].astype(F32)
    on = _rms_rows(o_scr[...], nw_ref[...])
    gated = (on * sg_ref[...].astype(F32)).astype(BF16)
    xn = x_ref[...] + jnp.dot(gated, wo_ref[...], preferred_element_type=F32)
    xy_ref[:, :D_MODEL] = xn
    xy_ref[:, D_MODEL:] = xn


def _hgrn(a3, l3, xy3, norm_w, w_out):
    b, t, _ = a3.shape
    tb = HGRN_BLOCK
    nb = t // tb
    d = D_MODEL
    state = pltpu.VMEM((HGRN_HEADS, HGRN_DK, HGRN_DK), F32)

    ob = pl.pallas_call(
        _hgrn_bwd_kernel,
        grid=(b, nb),
        in_specs=[
            pl.BlockSpec((None, tb, d), lambda bi, i: (bi, nb - 1 - i, 0)),
            pl.BlockSpec((None, tb, d), lambda bi, i: (bi, nb - 1 - i, 1)),
            pl.BlockSpec((None, tb, d), lambda bi, i: (bi, nb - 1 - i, 3)),
            pl.BlockSpec((None, tb, d), lambda bi, i: (bi, nb - 1 - i, 1)),
        ],
        out_specs=pl.BlockSpec((None, tb, d), lambda bi, i: (bi, nb - 1 - i, 0)),
        out_shape=jax.ShapeDtypeStruct((b, t, d), BF16),
        scratch_shapes=[state],
        compiler_params=_cparams(("arbitrary", "arbitrary")),
        name="hgrn_bwd",
    )(a3, a3, a3, l3)

    return pl.pallas_call(
        _hgrn_fwd_kernel,
        grid=(b, nb),
        in_specs=[
            pl.BlockSpec((None, tb, d), lambda bi, i: (bi, i, 0)),
            pl.BlockSpec((None, tb, d), lambda bi, i: (bi, i, 1)),
            pl.BlockSpec((None, tb, d), lambda bi, i: (bi, i, 2)),
            pl.BlockSpec((None, tb, d), lambda bi, i: (bi, i, 0)),
            pl.BlockSpec((None, tb, d), lambda bi, i: (bi, i, 4)),
            pl.BlockSpec((None, tb, d), lambda bi, i: (bi, i, 0)),
            pl.BlockSpec((None, tb, d), lambda bi, i: (bi, i, 1)),
            pl.BlockSpec((1, d), lambda bi, i: (0, 0)),
            pl.BlockSpec((d, d), lambda bi, i: (0, 0)),
        ],
        out_specs=pl.BlockSpec((None, tb, 2 * d), lambda bi, i: (bi, i, 0)),
        out_shape=jax.ShapeDtypeStruct((b, t, 2 * d), F32),
        scratch_shapes=[state, pltpu.VMEM((tb, d), F32)],
        compiler_params=_cparams(("arbitrary", "arbitrary")),
        name="hgrn_fwd",
    )(a3, a3, a3, l3, a3, ob, xy3, norm_w, w_out)


def _router_kernel(x_ref, nw_ref, wr_ref, aff_ref):
    xn = _rms_rows(x_ref[...], nw_ref[...])
    logits = lax.dot_general(wr_ref[...], xn, _NT, precision=lax.Precision.HIGHEST,
                             preferred_element_type=F32)
    m = jnp.max(logits, axis=0, keepdims=True)
    e = jnp.exp(logits - m)
    aff_ref[...] = e / jnp.sum(e, axis=0, keepdims=True)


def _router(xy, nw, wr_t):
    n = xy.shape[0]
    return pl.pallas_call(
        _router_kernel,
        grid=(n // PROJ_TM,),
        in_specs=[
            pl.BlockSpec((PROJ_TM, D_MODEL), lambda i: (i, 0)),
            pl.BlockSpec((1, D_MODEL), lambda i: (0, 0)),
            pl.BlockSpec((N_EXPERTS, D_MODEL), lambda i: (0, 0)),
        ],
        out_specs=pl.BlockSpec((N_EXPERTS, PROJ_TM), lambda i: (0, i)),
        out_shape=jax.ShapeDtypeStruct((N_EXPERTS, n), F32),
        compiler_params=_cparams(("arbitrary",)),
        name="router",
    )(xy, nw, wr_t)


def _ffn_kernel(idx_ref, idxn_ref, gate_ref, nw_ref, wg_ref, wu_ref, wd_ref, xy_in_ref,
                xy_ref, buf, stage, xn_scr, acc_scr, sem_g, sem_s, *, tiles_per_expert):
    del xy_in_ref
    tm = FFN_TM
    e = pl.program_id(0)
    i = pl.program_id(1)
    step = e * tiles_per_expert + i
    n_steps = pl.num_programs(0) * tiles_per_expert
    slot = step % 2
    first_tile = i == 0
    last_tile = i == tiles_per_expert - 1

    def start_gather(ids, dst_slot):
        def body(r, carry):
            t = ids[0, r]
            pltpu.make_async_copy(xy_ref.at[pl.ds(t, 1)], buf.at[dst_slot, pl.ds(r, 1)],
                                  sem_g.at[dst_slot]).start()
            return carry
        lax.fori_loop(0, tm, body, 0)

    def wait_gather(dst_slot):
        pltpu.make_async_copy(xy_ref.at[pl.ds(0, tm)], buf.at[dst_slot],
                              sem_g.at[dst_slot]).wait()

    def start_scatter():
        def body(r, carry):
            t = idx_ref[0, r]
            pltpu.make_async_copy(stage.at[pl.ds(r, 1)],
                                  xy_ref.at[pl.ds(t, 1), pl.ds(D_MODEL, D_MODEL)],
                                  sem_s.at[0]).start()
            return carry
        lax.fori_loop(0, tm, body, 0)

    def wait_scatter():
        pltpu.make_async_copy(stage, xy_ref.at[pl.ds(0, tm), pl.ds(D_MODEL, D_MODEL)],
                              sem_s.at[0]).wait()

    @pl.when(first_tile)
    def _():
        @pl.when(step > 0)
        def _():
            wait_scatter()
        start_gather(idx_ref, slot)

    @pl.when(jnp.logical_not(last_tile))
    def _():
        start_gather(idxn_ref, 1 - slot)

    wait_gather(slot)

    xn_scr[...] = _rms_rows(buf[slot, :, :D_MODEL], nw_ref[...]).astype(BF16)
    for f in range(EXPERT_FF // FFN_FW):
        fs = slice(f * FFN_FW, (f + 1) * FFN_FW)
        g = jnp.dot(xn_scr[...], wg_ref[:, fs], preferred_element_type=F32)
        u = jnp.dot(xn_scr[...], wu_ref[:, fs], preferred_element_type=F32)
        hcur = (g * jax.nn.sigmoid(g) * u).astype(BF16)
        part = jnp.dot(hcur, wd_ref[fs, :], preferred_element_type=F32)
        if f == 0:
            acc_scr[...] = part
        else:
            acc_scr[...] += part

    @pl.when(jnp.logical_not(first_tile))
    def _():
        wait_scatter()

    eye = (lax.broadcasted_iota(jnp.int32, (LANE, LANE), 0)
           == lax.broadcasted_iota(jnp.int32, (LANE, LANE), 1)).astype(F32)
    for r in range(tm // LANE):
        rows = slice(r * LANE, (r + 1) * LANE)
        gate_col = jnp.sum(eye * gate_ref[r:r + 1, :], axis=1, keepdims=True)
        stage[rows, :] = buf[slot, rows, D_MODEL:] + acc_scr[rows, :] * gate_col
    start_scatter()

    @pl.when(step == n_steps - 1)
    def _():
        wait_scatter()


def _expert_ffn(xy, idx, gate, nw, wg, wu, wd):
    n = xy.shape[0]
    cap = idx.shape[1]
    tm = FFN_TM
    nt = cap // tm
    n_steps = N_EXPERTS * nt
    idx3 = idx.reshape(n_steps, 1, tm)
    gate3 = gate.reshape(n_steps, tm // LANE, LANE)
    smem_tile = functools.partial(pl.BlockSpec, (None, 1, tm), memory_space=pltpu.SMEM)
    return pl.pallas_call(
        functools.partial(_ffn_kernel, tiles_per_expert=nt),
        grid=(N_EXPERTS, nt),
        in_specs=[
            smem_tile(lambda e, i: (e * nt + i, 0, 0)),
            smem_tile(lambda e, i: (jnp.minimum(e * nt + i + 1, n_steps - 1), 0, 0)),
            pl.BlockSpec((None, tm // LANE, LANE), lambda e, i: (e * nt + i, 0, 0)),
            pl.BlockSpec((1, D_MODEL), lambda e, i: (0, 0)),
            pl.BlockSpec((None, D_MODEL, EXPERT_FF), lambda e, i: (e, 0, 0)),
            pl.BlockSpec((None, D_MODEL, EXPERT_FF), lambda e, i: (e, 0, 0)),
            pl.BlockSpec((None, EXPERT_FF, D_MODEL), lambda e, i: (e, 0, 0)),
            pl.BlockSpec(memory_space=pl.ANY),
        ],
        out_specs=pl.BlockSpec(memory_space=pl.ANY),
        out_shape=jax.ShapeDtypeStruct((n, 2 * D_MODEL), F32),
        scratch_shapes=[
            pltpu.VMEM((2, tm, 2 * D_MODEL), F32),
            pltpu.VMEM((tm, D_MODEL), F32),
            pltpu.VMEM((tm, D_MODEL), BF16),
            pltpu.VMEM((tm, D_MODEL), F32),
            pltpu.SemaphoreType.DMA((2,)),
            pltpu.SemaphoreType.DMA((1,)),
        ],
        input_output_aliases={7: 0},
        compiler_params=_cparams(("arbitrary", "arbitrary")),
        name="expert_ffn",
    )(idx3, idx3, gate3, nw, wg, wu, wd, xy)


def _moe(xy, nw, wr_t, wg, wu, wd):
    n = xy.shape[0]
    cap = EC_CAPACITY_FACTOR * n // N_EXPERTS
    aff = _router(xy, nw, wr_t)
    gate, idx = lax.top_k(aff, cap)
    return _expert_ffn(xy, idx.astype(jnp.int32), gate, nw, wg, wu, wd)


def _final_norm_kernel(x_ref, nw_ref, o_ref):
    o_ref[...] = _rms_rows(x_ref[...], nw_ref[...])


def _final_norm(xy, nw):
    n = xy.shape[0]
    return pl.pallas_call(
        _final_norm_kernel,
        grid=(n // PROJ_TM,),
        in_specs=[pl.BlockSpec((PROJ_TM, D_MODEL), lambda i: (i, 1)),
                  pl.BlockSpec((1, D_MODEL), lambda i: (0, 0))],
        out_specs=pl.BlockSpec((PROJ_TM, D_MODEL), lambda i: (i, 0)),
        out_shape=jax.ShapeDtypeStruct((n, D_MODEL), F32),
        compiler_params=_cparams(("arbitrary",)),
        name="final_norm",
    )(xy, nw)


def _rope_tables(seq):
    half = RET_DK // 2
    inv = 1.0 / (ROPE_BASE ** (jnp.arange(half, dtype=F32) / half))
    ang = jnp.arange(seq, dtype=F32)[:, None] * inv[None, :]
    return jnp.cos(ang), jnp.sin(ang)


def _trunk(x, p):
    b, t, d = x.shape
    n = b * t
    cos, sin = _rope_tables(t)

    proj = _ret_inproj(x.reshape(n, d), 0, p["norm_mix"][0], p["ret_w_in"], cos, sin, t)
    xy = _retention(proj.reshape(b, t, -1), x, p["ret_decay"], p["ret_w_out"])
    xy = _moe(xy.reshape(n, 2 * d), p["norm_ffn"][0], p["w_router_t"][0],
              p["w_gate"][0], p["w_up"][0], p["w_down"][0])

    a, lf = _hgrn_inproj(xy, p["norm_mix"][1], p["hgrn_w_in"], p["hgrn_lb"])
    xy = _hgrn(a.reshape(b, t, -1), lf.reshape(b, t, -1), xy.reshape(b, t, 2 * d),
               p["hgrn_norm"], p["hgrn_w_out"])
    xy = _moe(xy.reshape(n, 2 * d), p["norm_ffn"][1], p["w_router_t"][1],
              p["w_gate"][1], p["w_up"][1], p["w_down"][1])

    return _final_norm(xy, p["norm_final"]).reshape(b, t, d)


def kernel(x_prompt, x_sample, norm_mix_w, norm_ffn_w, norm_final_w, ret_w_in, ret_w_out,
           hgrn_w_in, hgrn_lb, hgrn_norm_w, hgrn_w_out, moe_w_router, moe_w_gate, moe_w_up,
           moe_w_down):
    depth = norm_mix_w.shape[0]
    assert depth == 2 and ret_w_in.shape[0] == 1 and hgrn_w_in.shape[0] == 1
    sm = jax.nn.softmax(hgrn_lb.astype(F32), axis=0)
    lower_bounds = jnp.cumsum(sm, axis=0) - sm[0:1]
    p = {
        "norm_mix": norm_mix_w.reshape(depth, 1, D_MODEL),
        "norm_ffn": norm_ffn_w.reshape(depth, 1, D_MODEL),
        "norm_final": norm_final_w.reshape(1, D_MODEL),
        "ret_w_in": ret_w_in[0].astype(BF16),
        "ret_w_out": ret_w_out[0].astype(BF16),
        "ret_decay": _ret_decay_matrix(),
        "hgrn_w_in": hgrn_w_in[0].astype(BF16),
        "hgrn_lb": lower_bounds[1].reshape(1, 2 * D_MODEL),
        "hgrn_norm": hgrn_norm_w[0].reshape(1, D_MODEL),
        "hgrn_w_out": hgrn_w_out[0].astype(BF16),
        "w_router_t": jnp.swapaxes(moe_w_router, 1, 2),
        "w_gate": moe_w_gate.astype(BF16),
        "w_up": moe_w_up.astype(BF16),
        "w_down": moe_w_down.astype(BF16),
    }
    return _trunk(x_prompt, p), _trunk(x_sample, p)
```

```python
import functools
import math

import jax
import jax.numpy as jnp
from jax import lax
from jax.experimental import pallas as pl
from jax.experimental.pallas import tpu as pltpu

F32 = jnp.float32
BF16 = jnp.bfloat16

D_MODEL = 1024
NORM_EPS = 1e-6
ROPE_BASE = 10000.0

RET_HEADS = 4
RET_DK = 256
RET_DV = 512
RET_QK_WIDTH = RET_HEADS * RET_DK
RET_V_WIDTH = RET_HEADS * RET_DV
RET_DECAY_FWD = 5.0
RET_DECAY_BWD = 5.5
RET_BLOCK = 256

HGRN_HEADS = 8
HGRN_DK = 128
HGRN_CHUNK = 64
HGRN_BLOCK = 256

N_EXPERTS = 16
EC_CAPACITY_FACTOR = 2
EXPERT_FF = 2 * D_MODEL

PROJ_TM = 512
PROJ_CW = 512
FFN_TM = 512
FFN_FW = 512
LANE = 128
X_ROWS = D_MODEL // LANE
TOK_ROWS = 2 * X_ROWS
FFN_PITCH = 24

VMEM_LIMIT_BYTES = 52 * 1024 * 1024


def _cparams(sem):
    return pltpu.CompilerParams(dimension_semantics=sem, vmem_limit_bytes=VMEM_LIMIT_BYTES)


def _ret_log_gamma(offset):
    return [math.log1p(-(2.0 ** (-offset - h))) for h in range(RET_HEADS)]


def _rms_rows(x, w):
    ms = jnp.mean(x * x, axis=-1, keepdims=True)
    return x * lax.rsqrt(ms + NORM_EPS) * w


def _load_tok(ref, first_row, n_tok, pitch=TOK_ROWS):
    return jnp.concatenate(
        [ref[pl.ds(first_row + j, n_tok, stride=pitch), :] for j in range(X_ROWS)], axis=-1)


def _store_tok(ref, first_row, val, pitch=TOK_ROWS):
    for j in range(X_ROWS):
        ref[pl.ds(first_row + j, val.shape[0], stride=pitch), :] = val[:, j * LANE:(j + 1) * LANE]


def _ret_inproj_kernel(x_ref, nw_ref, w_ref, cos_ref, sin_ref, o_ref, xn_scr):
    xn_scr[...] = _rms_rows(x_ref[...], nw_ref[...]).astype(BF16)
    cos = cos_ref[...]
    sin = sin_ref[...]
    width = w_ref.shape[1]
    half = RET_DK // 2
    for j in range(width // PROJ_CW):
        c0 = j * PROJ_CW
        r = jnp.dot(xn_scr[...], w_ref[:, c0:c0 + PROJ_CW], preferred_element_type=F32)
        if c0 < 2 * RET_QK_WIDTH:
            scale = 1.0 if c0 < RET_QK_WIDTH else RET_DK ** -0.5
            parts = []
            for hh in range(PROJ_CW // RET_DK):
                x1 = r[:, hh * RET_DK:hh * RET_DK + half]
                x2 = r[:, hh * RET_DK + half:(hh + 1) * RET_DK]
                parts.append((x1 * cos - x2 * sin) * scale)
                parts.append((x1 * sin + x2 * cos) * scale)
            r = jnp.concatenate(parts, axis=-1)
        elif c0 >= 2 * RET_QK_WIDTH + RET_V_WIDTH:
            r = r * jax.nn.sigmoid(r)
        o_ref[:, c0:c0 + PROJ_CW] = r.astype(BF16)


def _ret_inproj(x2d, col, nw, w, cos, sin, seq):
    n = x2d.shape[0]
    width = w.shape[1]
    tpb = seq // PROJ_TM
    return pl.pallas_call(
        _ret_inproj_kernel,
        grid=(n // PROJ_TM,),
        in_specs=[
            pl.BlockSpec((PROJ_TM, D_MODEL), lambda i: (i, col)),
            pl.BlockSpec((1, D_MODEL), lambda i: (0, 0)),
            pl.BlockSpec((D_MODEL, width), lambda i: (0, 0)),
            pl.BlockSpec((PROJ_TM, RET_DK // 2), lambda i: (i % tpb, 0)),
            pl.BlockSpec((PROJ_TM, RET_DK // 2), lambda i: (i % tpb, 0)),
        ],
        out_specs=pl.BlockSpec((PROJ_TM, width), lambda i: (i, 0)),
        out_shape=jax.ShapeDtypeStruct((n, width), BF16),
        scratch_shapes=[pltpu.VMEM((PROJ_TM, D_MODEL), BF16)],
        compiler_params=_cparams(("arbitrary",)),
        name="ret_inproj",
    )(x2d, nw, w, cos, sin)


def _hgrn_inproj_kernel(xy_ref, nw_ref, w_ref, lb_ref, a_ref, l_ref, xn_scr):
    xn_scr[...] = _rms_rows(_load_tok(xy_ref, X_ROWS, PROJ_TM), nw_ref[...]).astype(BF16)
    width = w_ref.shape[1]
    for j in range(width // PROJ_CW):
        c0 = j * PROJ_CW
        r = jnp.dot(xn_scr[...], w_ref[:, c0:c0 + PROJ_CW], preferred_element_type=F32)
        if c0 < D_MODEL:
            a_ref[:, c0:c0 + PROJ_CW] = (r * jax.nn.sigmoid(r)).astype(BF16)
        elif c0 < 2 * D_MODEL:
            a_ref[:, c0:c0 + PROJ_CW] = r.astype(BF16)
        elif c0 < 4 * D_MODEL:
            g0 = c0 - 2 * D_MODEL
            lb = lb_ref[:, g0:g0 + PROJ_CW]
            f = lb + (1.0 - lb) * jax.nn.sigmoid(r)
            a_ref[:, c0:c0 + PROJ_CW] = (1.0 - f).astype(BF16)
            l_ref[:, g0:g0 + PROJ_CW] = jnp.log(f)
        else:
            a_ref[:, c0:c0 + PROJ_CW] = jax.nn.sigmoid(r).astype(BF16)


def _hgrn_inproj(xy, nw, w, lb):
    n = xy.shape[0] // TOK_ROWS
    width = w.shape[1]
    return pl.pallas_call(
        _hgrn_inproj_kernel,
        grid=(n // PROJ_TM,),
        in_specs=[
            pl.BlockSpec((PROJ_TM * TOK_ROWS, LANE), lambda i: (i, 0)),
            pl.BlockSpec((1, D_MODEL), lambda i: (0, 0)),
            pl.BlockSpec((D_MODEL, width), lambda i: (0, 0)),
            pl.BlockSpec((1, 2 * D_MODEL), lambda i: (0, 0)),
        ],
        out_specs=[
            pl.BlockSpec((PROJ_TM, width), lambda i: (i, 0)),
            pl.BlockSpec((PROJ_TM, 2 * D_MODEL), lambda i: (i, 0)),
        ],
        out_shape=[
            jax.ShapeDtypeStruct((n, width), BF16),
            jax.ShapeDtypeStruct((n, 2 * D_MODEL), F32),
        ],
        scratch_shapes=[pltpu.VMEM((PROJ_TM, D_MODEL), BF16)],
        compiler_params=_cparams(("arbitrary",)),
        name="hgrn_inproj",
    )(xy, nw, w, lb)


_NT = (((1,), (1,)), ((), ()))
_TN = (((0,), (0,)), ((), ()))


def _ret_bwd_kernel(q_ref, k_ref, v_ref, ob_ref, r_scr, *, lg):
    @pl.when(pl.program_id(1) == 0)
    def _():
        r_scr[...] = jnp.zeros_like(r_scr)

    c = q_ref.shape[0]
    pos = lax.broadcasted_iota(jnp.int32, (c, 1), 0).astype(F32)
    for h in range(RET_HEADS):
        qh = q_ref[:, h * RET_DK:(h + 1) * RET_DK].astype(F32)
        kh = k_ref[:, h * RET_DK:(h + 1) * RET_DK].astype(F32)
        vh = v_ref[:, h * RET_DV:(h + 1) * RET_DV]
        qd = (qh * jnp.exp((c - pos) * lg[h])).astype(BF16)
        o = jnp.dot(qd, r_scr[h].astype(BF16), preferred_element_type=F32)
        ob_ref[:, h * RET_DV:(h + 1) * RET_DV] = o.astype(BF16)
        kd = (kh * jnp.exp(pos * lg[h])).astype(BF16)
        r_scr[h] = r_scr[h] * math.exp(c * lg[h]) + lax.dot_general(
            kd, vh, _TN, preferred_element_type=F32)


def _ret_fwd_kernel(q_ref, k_ref, v_ref, g_ref, ob_ref, x_ref, m_ref, wo_ref, xy_ref,
                    r_scr, *, lg):
    @pl.when(pl.program_id(1) == 0)
    def _():
        r_scr[...] = jnp.zeros_like(r_scr)

    c = q_ref.shape[0]
    pos = lax.broadcasted_iota(jnp.int32, (c, 1), 0).astype(F32)
    acc = x_ref[...]
    for h in range(RET_HEADS):
        qb = q_ref[:, h * RET_DK:(h + 1) * RET_DK]
        kb = k_ref[:, h * RET_DK:(h + 1) * RET_DK]
        vh = v_ref[:, h * RET_DV:(h + 1) * RET_DV]
        s = lax.dot_general(qb, kb, _NT, preferred_element_type=F32)
        sm = (s * m_ref[h]).astype(BF16)
        qd = (qb.astype(F32) * jnp.exp((pos + 1.0) * lg[h])).astype(BF16)
        o = (jnp.dot(sm, vh, preferred_element_type=F32)
             + jnp.dot(qd, r_scr[h].astype(BF16), preferred_element_type=F32)
             + ob_ref[:, h * RET_DV:(h + 1) * RET_DV].astype(F32))
        kd = (kb.astype(F32) * jnp.exp((c - 1.0 - pos) * lg[h])).astype(BF16)
        r_scr[h] = r_scr[h] * math.exp(c * lg[h]) + lax.dot_general(
            kd, vh, _TN, preferred_element_type=F32)
        ms = jnp.mean(o * o, axis=-1, keepdims=True)
        gated = (o * lax.rsqrt(ms + NORM_EPS)
                 * g_ref[:, h * RET_DV:(h + 1) * RET_DV].astype(F32)).astype(BF16)
        acc = acc + jnp.dot(gated, wo_ref[h * RET_DV:(h + 1) * RET_DV, :],
                            preferred_element_type=F32)
    _store_tok(xy_ref, 0, acc)
    _store_tok(xy_ref, X_ROWS, acc)


def _retention(proj, x3, decay_mat, w_out):
    b, t, _ = proj.shape
    c = RET_BLOCK
    nc = t // c
    lgf = _ret_log_gamma(RET_DECAY_FWD)
    lgb = _ret_log_gamma(RET_DECAY_BWD)
    qk = RET_QK_WIDTH
    vw = RET_V_WIDTH
    state = pltpu.VMEM((RET_HEADS, RET_DK, RET_DV), F32)

    ob = pl.pallas_call(
        functools.partial(_ret_bwd_kernel, lg=lgb),
        grid=(b, nc),
        in_specs=[
            pl.BlockSpec((None, c, qk), lambda bi, i: (bi, nc - 1 - i, 0)),
            pl.BlockSpec((None, c, qk), lambda bi, i: (bi, nc - 1 - i, 1)),
            pl.BlockSpec((None, c, vw), lambda bi, i: (bi, nc - 1 - i, 1)),
        ],
        out_specs=pl.BlockSpec((None, c, vw), lambda bi, i: (bi, nc - 1 - i, 0)),
        out_shape=jax.ShapeDtypeStruct((b, t, vw), BF16),
        scratch_shapes=[state],
        compiler_params=_cparams(("arbitrary", "arbitrary")),
        name="ret_bwd",
    )(proj, proj, proj)

    return pl.pallas_call(
        functools.partial(_ret_fwd_kernel, lg=lgf),
        grid=(b, nc),
        in_specs=[
            pl.BlockSpec((None, c, qk), lambda bi, i: (bi, i, 0)),
            pl.BlockSpec((None, c, qk), lambda bi, i: (bi, i, 1)),
            pl.BlockSpec((None, c, vw), lambda bi, i: (bi, i, 1)),
            pl.BlockSpec((None, c, vw), lambda bi, i: (bi, i, 2)),
            pl.BlockSpec((None, c, vw), lambda bi, i: (bi, i, 0)),
            pl.BlockSpec((None, c, D_MODEL), lambda bi, i: (bi, i, 0)),
            pl.BlockSpec((RET_HEADS, c, c), lambda bi, i: (0, 0, 0)),
            pl.BlockSpec((vw, D_MODEL), lambda bi, i: (0, 0)),
        ],
        out_specs=pl.BlockSpec((None, c * TOK_ROWS, LANE), lambda bi, i: (bi, i, 0)),
        out_shape=jax.ShapeDtypeStruct((b, t * TOK_ROWS, LANE), F32),
        scratch_shapes=[state],
        compiler_params=_cparams(("arbitrary", "arbitrary")),
        name="ret_fwd",
    )(proj, proj, proj, proj, ob, x3, decay_mat, w_out)


def _ret_decay_matrix():
    c = RET_BLOCK
    rel = (jnp.arange(c, dtype=F32)[:, None] - jnp.arange(c, dtype=F32)[None, :])[None]
    lgf = jnp.asarray(_ret_log_gamma(RET_DECAY_FWD), F32)[:, None, None]
    lgb = jnp.asarray(_ret_log_gamma(RET_DECAY_BWD), F32)[:, None, None]
    return jnp.where(rel >= 0, jnp.exp(jnp.maximum(rel, 0.0) * lgf),
                     jnp.exp(jnp.maximum(-rel, 0.0) * lgb))


def _tri_cumsum(tri, x):
    hi = x.astype(BF16)
    r1 = x - hi.astype(F32)
    mid = r1.astype(BF16)
    lo = (r1 - mid.astype(F32)).astype(BF16)
    return (jnp.dot(tri, hi, preferred_element_type=F32)
            + jnp.dot(tri, mid, preferred_element_type=F32)
            + jnp.dot(tri, lo, preferred_element_type=F32))


def _hgrn_chunk(q, k, v, lf, s_scr, reverse):
    c = q.shape[0]
    row = lax.broadcasted_iota(jnp.int32, (c, c), 0)
    col = lax.broadcasted_iota(jnp.int32, (c, c), 1)
    keep = (col >= row) if reverse else (col <= row)
    g = _tri_cumsum(keep.astype(BF16), lf)
    g_last = g[0:1, :] if reverse else g[c - 1:c, :]
    qg = (q.astype(F32) * jnp.exp(g)).astype(BF16)
    kf = k.astype(F32)
    kg = (kf * jnp.exp(-g)).astype(BF16)
    kd = (kf * jnp.exp(g_last - g)).astype(BF16)
    e_last = jnp.exp(g_last)
    outs = []
    for h in range(HGRN_HEADS):
        cs = slice(h * HGRN_DK, (h + 1) * HGRN_DK)
        s = lax.dot_general(qg[:, cs], kg[:, cs], _NT, preferred_element_type=F32)
        s = jnp.where(keep, s, 0.0).astype(BF16)
        st = s_scr[h]
        o = (jnp.dot(s, v[:, cs], preferred_element_type=F32)
             + lax.dot_general(qg[:, cs], st.astype(BF16), _NT, preferred_element_type=F32))
        s_scr[h] = st * e_last[:, cs] + lax.dot_general(
            v[:, cs], kd[:, cs], _TN, preferred_element_type=F32)
        outs.append(o)
    return jnp.concatenate(outs, axis=-1)


def _hgrn_bwd_kernel(q_ref, v_ref, k_ref, lf_ref, ob_ref, s_scr):
    @pl.when(pl.program_id(1) == 0)
    def _():
        s_scr[...] = jnp.zeros_like(s_scr)

    for sc in reversed(range(HGRN_BLOCK // HGRN_CHUNK)):
        rows = slice(sc * HGRN_CHUNK, (sc + 1) * HGRN_CHUNK)
        o = _hgrn_chunk(q_ref[rows, :], k_ref[rows, :], v_ref[rows, :], lf_ref[rows, :],
                        s_scr, reverse=True)
        ob_ref[rows, :] = o.astype(BF16)


def _hgrn_fwd_kernel(q_ref, v_ref, k_ref, lf_ref, sg_ref, ob_ref, x_ref, nw_ref, wo_ref,
                     xy_ref, s_scr, o_scr):
    @pl.when(pl.program_id(1) == 0)
    def _():
        s_scr[...] = jnp.zeros_like(s_scr)

    for sc in range(HGRN_BLOCK // HGRN_CHUNK):
        rows = slice(sc * HGRN_CHUNK, (sc + 1) * HGRN_CHUNK)
        o = _hgrn_chunk(q_ref[rows, :], k_ref[rows, :], v_ref[rows, :], lf_ref[rows, :],
                        s_scr, reverse=False)
        o_scr[rows, :] = o + ob_ref[rows, :].astype(F32)
    on = _rms_rows(o_scr[...], nw_ref[...])
    gated = (on * sg_ref[...].astype(F32)).astype(BF16)
    xn = (_load_tok(x_ref, X_ROWS, HGRN_BLOCK)
          + jnp.dot(gated, wo_ref[...], preferred_element_type=F32))
    _store_tok(xy_ref, 0, xn)
    _store_tok(xy_ref, X_ROWS, xn)


def _hgrn(a3, l3, xy3, norm_w, w_out):
    b, t, _ = a3.shape
    tb = HGRN_BLOCK
    nb = t // tb
    d = D_MODEL
    state = pltpu.VMEM((HGRN_HEADS, HGRN_DK, HGRN_DK), F32)

    ob = pl.pallas_call(
        _hgrn_bwd_kernel,
        grid=(b, nb),
        in_specs=[
            pl.BlockSpec((None, tb, d), lambda bi, i: (bi, nb - 1 - i, 0)),
            pl.BlockSpec((None, tb, d), lambda bi, i: (bi, nb - 1 - i, 1)),
            pl.BlockSpec((None, tb, d), lambda bi, i: (bi, nb - 1 - i, 3)),
            pl.BlockSpec((None, tb, d), lambda bi, i: (bi, nb - 1 - i, 1)),
        ],
        out_specs=pl.BlockSpec((None, tb, d), lambda bi, i: (bi, nb - 1 - i, 0)),
        out_shape=jax.ShapeDtypeStruct((b, t, d), BF16),
        scratch_shapes=[state],
        compiler_params=_cparams(("arbitrary", "arbitrary")),
        name="hgrn_bwd",
    )(a3, a3, a3, l3)

    return pl.pallas_call(
        _hgrn_fwd_kernel,
        grid=(b, nb),
        in_specs=[
            pl.BlockSpec((None, tb, d), lambda bi, i: (bi, i, 0)),
            pl.BlockSpec((None, tb, d), lambda bi, i: (bi, i, 1)),
            pl.BlockSpec((None, tb, d), lambda bi, i: (bi, i, 2)),
            pl.BlockSpec((None, tb, d), lambda bi, i: (bi, i, 0)),
            pl.BlockSpec((None, tb, d), lambda bi, i: (bi, i, 4)),
            pl.BlockSpec((None, tb, d), lambda bi, i: (bi, i, 0)),
            pl.BlockSpec((None, tb * TOK_ROWS, LANE), lambda bi, i: (bi, i, 0)),
            pl.BlockSpec((1, d), lambda bi, i: (0, 0)),
            pl.BlockSpec((d, d), lambda bi, i: (0, 0)),
        ],
        out_specs=pl.BlockSpec((None, tb * TOK_ROWS, LANE), lambda bi, i: (bi, i, 0)),
        out_shape=jax.ShapeDtypeStruct((b, t * TOK_ROWS, LANE), F32),
        scratch_shapes=[state, pltpu.VMEM((tb, d), F32)],
        compiler_params=_cparams(("arbitrary", "arbitrary")),
        name="hgrn_fwd",
    )(a3, a3, a3, l3, a3, ob, xy3, norm_w, w_out)


def _router_kernel(xy_ref, nw_ref, wr_ref, aff_ref):
    xn = _rms_rows(_load_tok(xy_ref, 0, PROJ_TM), nw_ref[...])
    logits = lax.dot_general(wr_ref[...], xn, _NT, precision=lax.Precision.HIGHEST,
                             preferred_element_type=F32)
    m = jnp.max(logits, axis=0, keepdims=True)
    e = jnp.exp(logits - m)
    aff_ref[...] = e / jnp.sum(e, axis=0, keepdims=True)


def _router(xy, nw, wr_t):
    n = xy.shape[0] // TOK_ROWS
    return pl.pallas_call(
        _router_kernel,
        grid=(n // PROJ_TM,),
        in_specs=[
            pl.BlockSpec((PROJ_TM * TOK_ROWS, LANE), lambda i: (i, 0)),
            pl.BlockSpec((1, D_MODEL), lambda i: (0, 0)),
            pl.BlockSpec((N_EXPERTS, D_MODEL), lambda i: (0, 0)),
        ],
        out_specs=pl.BlockSpec((N_EXPERTS, PROJ_TM), lambda i: (0, i)),
        out_shape=jax.ShapeDtypeStruct((N_EXPERTS, n), F32),
        compiler_params=_cparams(("arbitrary",)),
        name="router",
    )(xy, nw, wr_t)


def _ffn_kernel(idx_ref, idxf_ref, idxp_ref, idxn_ref, gate_ref, nw_ref, wg_ref, wu_ref, wd_ref,
                xy_in_ref, xy_ref, buf, work, xn_scr, sem_g, sem_s, *, tiles_per_expert):
    del xy_in_ref
    tm = FFN_TM
    pitch = FFN_PITCH
    e = pl.program_id(0)
    i = pl.program_id(1)
    step = e * tiles_per_expert + i
    n_steps = pl.num_programs(0) * tiles_per_expert
    slot = step % 2
    last_tile = i == tiles_per_expert - 1
    base = pl.multiple_of(slot * (tm * pitch), 8)
    st_cur = pl.multiple_of(slot * (tm * X_ROWS), 8)
    st_prev = pl.multiple_of((1 - slot) * (tm * X_ROWS), 8)
    acc0 = 2 * tm * X_ROWS

    def start_gather(ids, dst_slot, lo=0, hi=FFN_TM):
        dst = pl.multiple_of(dst_slot * (tm * pitch), 8)
        for r in range(lo, hi):
            t = pl.multiple_of(ids[0, r] * TOK_ROWS, TOK_ROWS)
            pltpu.make_async_copy(xy_ref.at[pl.ds(t, TOK_ROWS), :],
                                  buf.at[pl.ds(dst + r * pitch, TOK_ROWS), :],
                                  sem_g.at[dst_slot]).start()

    def wait_gather(dst_slot):
        pltpu.make_async_copy(xy_ref.at[pl.ds(0, tm * TOK_ROWS), :],
                              buf.at[pl.ds(0, tm * TOK_ROWS), :], sem_g.at[dst_slot]).wait()

    def start_scatter(ids, src, lo=0, hi=FFN_TM):
        for r in range(lo, hi):
            t = pl.multiple_of(ids[0, r] * TOK_ROWS + X_ROWS, X_ROWS)
            pltpu.make_async_copy(work.at[pl.ds(src + r * X_ROWS, X_ROWS), :],
                                  xy_ref.at[pl.ds(t, X_ROWS), :], sem_s.at[0]).start()

    def wait_scatter():
        pltpu.make_async_copy(work.at[pl.ds(0, tm * X_ROWS), :],
                              xy_ref.at[pl.ds(0, tm * X_ROWS), :], sem_s.at[0]).wait()

    @pl.when(step == 0)
    def _():
        start_gather(idx_ref, 0)

    wait_gather(slot)

    @pl.when(i == 0)
    def _():
        for j in range(X_ROWS):
            work[pl.ds(st_prev + j, tm, stride=X_ROWS), :] = (
                buf[pl.ds(base + X_ROWS + j, tm, stride=pitch), :])

    xn_scr[...] = _rms_rows(_load_tok(buf, base, tm, pitch), nw_ref[...]).astype(BF16)

    n_f = EXPERT_FF // FFN_FW
    n_pts = 3 * n_f
    cuts = [(tm * k) // n_pts for k in range(n_pts + 1)]

    def issue(k):
        start_scatter(idxp_ref, st_prev, cuts[k], cuts[k + 1])
        start_gather(idxf_ref, 1 - slot, cuts[k], cuts[k + 1])

    eye = (lax.broadcasted_iota(jnp.int32, (LANE, LANE), 0)
           == lax.broadcasted_iota(jnp.int32, (LANE, LANE), 1)).astype(F32)
    gate_col = jnp.concatenate(
        [jnp.sum(eye * gate_ref[r:r + 1, :], axis=1, keepdims=True) for r in range(tm // LANE)],
        axis=0)
    for f in range(n_f):
        fs = slice(f * FFN_FW, (f + 1) * FFN_FW)
        issue(3 * f)
        g = jnp.dot(xn_scr[...], wg_ref[:, fs], preferred_element_type=F32)
        issue(3 * f + 1)
        u = jnp.dot(xn_scr[...], wu_ref[:, fs], preferred_element_type=F32)
        hcur = (g * jax.nn.sigmoid(g) * u * gate_col).astype(BF16)
        issue(3 * f + 2)
        part = jnp.dot(hcur, wd_ref[fs, :], preferred_element_type=F32)
        if f < n_f - 1:
            for j in range(X_ROWS):
                rows = pl.ds(acc0 + j * tm, tm)
                cs = slice(j * LANE, (j + 1) * LANE)
                if f == 0:
                    work[rows, :] = part[:, cs]
                elif f == 1:
                    work[rows, :] = (work[rows, :] + part[:, cs]
                                     + buf[pl.ds(base + X_ROWS + j, tm, stride=pitch), :])
                else:
                    work[rows, :] += part[:, cs]
        else:
            wait_scatter()
            for j in range(X_ROWS):
                cs = slice(j * LANE, (j + 1) * LANE)
                work[pl.ds(st_cur + j, tm, stride=X_ROWS), :] = (
                    work[pl.ds(acc0 + j * tm, tm), :] + part[:, cs])

    @pl.when(last_tile)
    def _():
        wait_gather(1 - slot)
        start_scatter(idx_ref, st_cur)
        wait_scatter()

        @pl.when(step < n_steps - 1)
        def _():
            start_gather(idxn_ref, 1 - slot)


def _expert_ffn(xy, idx, gate, nw, wg, wu, wd):
    cap = idx.shape[1]
    tm = FFN_TM
    nt = cap // tm
    assert nt >= 2 and EXPERT_FF // FFN_FW >= 3
    n_steps = N_EXPERTS * nt
    idx3 = idx.reshape(n_steps, 1, tm)
    gate3 = gate.reshape(n_steps, tm // LANE, LANE)
    smem_tile = functools.partial(pl.BlockSpec, (None, 1, tm), memory_space=pltpu.SMEM)
    return pl.pallas_call(
        functools.partial(_ffn_kernel, tiles_per_expert=nt),
        grid=(N_EXPERTS, nt),
        in_specs=[
            smem_tile(lambda e, i: (e * nt + i, 0, 0)),
            smem_tile(lambda e, i: (e * nt + jnp.minimum(i + 1, nt - 1), 0, 0)),
            smem_tile(lambda e, i: (e * nt + jnp.maximum(i - 1, 0), 0, 0)),
            smem_tile(lambda e, i: (jnp.minimum(e * nt + i + 1, n_steps - 1), 0, 0)),
            pl.BlockSpec((None, tm // LANE, LANE), lambda e, i: (e * nt + i, 0, 0)),
            pl.BlockSpec((1, D_MODEL), lambda e, i: (0, 0)),
            pl.BlockSpec((None, D_MODEL, EXPERT_FF), lambda e, i: (e, 0, 0)),
            pl.BlockSpec((None, D_MODEL, EXPERT_FF), lambda e, i: (e, 0, 0)),
            pl.BlockSpec((None, EXPERT_FF, D_MODEL), lambda e, i: (e, 0, 0)),
            pl.BlockSpec(memory_space=pl.ANY),
        ],
        out_specs=pl.BlockSpec(memory_space=pl.ANY),
        out_shape=jax.ShapeDtypeStruct(xy.shape, F32),
        scratch_shapes=[
            pltpu.VMEM((2 * tm * FFN_PITCH, LANE), F32),
            pltpu.VMEM((3 * tm * X_ROWS, LANE), F32),
            pltpu.VMEM((tm, D_MODEL), BF16),
            pltpu.SemaphoreType.DMA((2,)),
            pltpu.SemaphoreType.DMA((1,)),
        ],
        input_output_aliases={9: 0},
        compiler_params=_cparams(("arbitrary", "arbitrary")),
        name="expert_ffn",
    )(idx3, idx3, idx3, idx3, gate3, nw, wg, wu, wd, xy)


def _moe(xy, nw, wr_t, wg, wu, wd):
    n = xy.shape[0] // TOK_ROWS
    cap = EC_CAPACITY_FACTOR * n // N_EXPERTS
    aff = _router(xy, nw, wr_t)
    gate, idx = lax.top_k(aff, cap)
    return _expert_ffn(xy, idx.astype(jnp.int32), gate, nw, wg, wu, wd)


def _final_norm_kernel(xy_ref, nw_ref, o_ref):
    o_ref[...] = _rms_rows(_load_tok(xy_ref, X_ROWS, PROJ_TM), nw_ref[...])


def _final_norm(xy, nw):
    n = xy.shape[0] // TOK_ROWS
    return pl.pallas_call(
        _final_norm_kernel,
        grid=(n // PROJ_TM,),
        in_specs=[pl.BlockSpec((PROJ_TM * TOK_ROWS, LANE), lambda i: (i, 0)),
                  pl.BlockSpec((1, D_MODEL), lambda i: (0, 0))],
        out_specs=pl.BlockSpec((PROJ_TM, D_MODEL), lambda i: (i, 0)),
        out_shape=jax.ShapeDtypeStruct((n, D_MODEL), F32),
        compiler_params=_cparams(("arbitrary",)),
        name="final_norm",
    )(xy, nw)


def _rope_tables(seq):
    half = RET_DK // 2
    inv = 1.0 / (ROPE_BASE ** (jnp.arange(half, dtype=F32) / half))
    ang = jnp.arange(seq, dtype=F32)[:, None] * inv[None, :]
    return jnp.cos(ang), jnp.sin(ang)


def _trunk(x, p):
    b, t, d = x.shape
    n = b * t
    cos, sin = _rope_tables(t)

    proj = _ret_inproj(x.reshape(n, d), 0, p["norm_mix"][0], p["ret_w_in"], cos, sin, t)
    xy = _retention(proj.reshape(b, t, -1), x, p["ret_decay"], p["ret_w_out"])
    xy = _moe(xy.reshape(n * TOK_ROWS, LANE), p["norm_ffn"][0], p["w_router_t"][0],
              p["w_gate"][0], p["w_up"][0], p["w_down"][0])

    a, lf = _hgrn_inproj(xy, p["norm_mix"][1], p["hgrn_w_in"], p["hgrn_lb"])
    xy = _hgrn(a.reshape(b, t, -1), lf.reshape(b, t, -1), xy.reshape(b, t * TOK_ROWS, LANE),
               p["hgrn_norm"], p["hgrn_w_out"])
    xy = _moe(xy.reshape(n * TOK_ROWS, LANE), p["norm_ffn"][1], p["w_router_t"][1],
              p["w_gate"][1], p["w_up"][1], p["w_down"][1])

    return _final_norm(xy, p["norm_final"]).reshape(b, t, d)


def kernel(x_prompt, x_sample, norm_mix_w, norm_ffn_w, norm_final_w, ret_w_in, ret_w_out,
           hgrn_w_in, hgrn_lb, hgrn_norm_w, hgrn_w_out, moe_w_router, moe_w_gate, moe_w_up,
           moe_w_down):
    depth = norm_mix_w.shape[0]
    assert depth == 2 and ret_w_in.shape[0] == 1 and hgrn_w_in.shape[0] == 1
    sm = jax.nn.softmax(hgrn_lb.astype(F32), axis=0)
    lower_bounds = jnp.cumsum(sm, axis=0) - sm[0:1]
    p = {
        "norm_mix": norm_mix_w.reshape(depth, 1, D_MODEL),
        "norm_ffn": norm_ffn_w.reshape(depth, 1, D_MODEL),
        "norm_final": norm_final_w.reshape(1, D_MODEL),
        "ret_w_in": ret_w_in[0].astype(BF16),
        "ret_w_out": ret_w_out[0].astype(BF16),
        "ret_decay": _ret_decay_matrix(),
        "hgrn_w_in": hgrn_w_in[0].astype(BF16),
        "hgrn_lb": lower_bounds[1].reshape(1, 2 * D_MODEL),
        "hgrn_norm": hgrn_norm_w[0].reshape(1, D_MODEL),
        "hgrn_w_out": hgrn_w_out[0].astype(BF16),
        "w_router_t": jnp.swapaxes(moe_w_router, 1, 2),
        "w_gate": moe_w_gate.astype(BF16),
        "w_up": moe_w_up.astype(BF16),
        "w_down": moe_w_down.astype(BF16),
    }
    return _trunk(x_prompt, p), _trunk(x_sample, p)
```

```python
import functools
import math

import jax
import jax.numpy as jnp
from jax import lax
from jax.experimental import pallas as pl
from jax.experimental.pallas import tpu as pltpu
from jax.experimental.pallas import tpu_sc as plsc

F32 = jnp.float32
I32 = jnp.int32
BF16 = jnp.bfloat16

D_MODEL = 1024
NORM_EPS = 1e-6
ROPE_BASE = 10000.0

RET_HEADS = 4
RET_DK = 256
RET_DV = 512
RET_QK_WIDTH = RET_HEADS * RET_DK
RET_V_WIDTH = RET_HEADS * RET_DV
RET_DECAY_FWD = 5.0
RET_DECAY_BWD = 5.5
RET_BLOCK = 256

HGRN_HEADS = 8
HGRN_DK = 128
HGRN_CHUNK = 64
HGRN_BLOCK = 256

N_EXPERTS = 16
EC_CAPACITY_FACTOR = 2
EXPERT_FF = 2 * D_MODEL

PROJ_TM = 512
PROJ_CW = 512
FFN_TM = 512
FFN_FW = 512
LANE = 128
SC_LANES = 16
X_ROWS = D_MODEL // LANE
TOK_ROWS = 2 * X_ROWS
FFN_PITCH = 24

VMEM_LIMIT_BYTES = 52 * 1024 * 1024


def _cparams(sem):
    return pltpu.CompilerParams(dimension_semantics=sem, vmem_limit_bytes=VMEM_LIMIT_BYTES)


def _ret_log_gamma(offset):
    return [math.log1p(-(2.0 ** (-offset - h))) for h in range(RET_HEADS)]


def _rms_rows(x, w):
    ms = jnp.mean(x * x, axis=-1, keepdims=True)
    return x * lax.rsqrt(ms + NORM_EPS) * w


def _load_tok(ref, first_row, n_tok, pitch=TOK_ROWS):
    return jnp.concatenate(
        [ref[pl.ds(first_row + j, n_tok, stride=pitch), :] for j in range(X_ROWS)], axis=-1)


def _store_tok(ref, first_row, val, pitch=TOK_ROWS):
    for j in range(X_ROWS):
        ref[pl.ds(first_row + j, val.shape[0], stride=pitch), :] = val[:, j * LANE:(j + 1) * LANE]


def _ret_inproj_kernel(x_ref, nw_ref, w_ref, cos_ref, sin_ref, o_ref, xn_scr):
    xn_scr[...] = _rms_rows(x_ref[...], nw_ref[...]).astype(BF16)
    cos = cos_ref[...]
    sin = sin_ref[...]
    width = w_ref.shape[1]
    half = RET_DK // 2
    for j in range(width // PROJ_CW):
        c0 = j * PROJ_CW
        r = jnp.dot(xn_scr[...], w_ref[:, c0:c0 + PROJ_CW], preferred_element_type=F32)
        if c0 < 2 * RET_QK_WIDTH:
            scale = 1.0 if c0 < RET_QK_WIDTH else RET_DK ** -0.5
            parts = []
            for hh in range(PROJ_CW // RET_DK):
                x1 = r[:, hh * RET_DK:hh * RET_DK + half]
                x2 = r[:, hh * RET_DK + half:(hh + 1) * RET_DK]
                parts.append((x1 * cos - x2 * sin) * scale)
                parts.append((x1 * sin + x2 * cos) * scale)
            r = jnp.concatenate(parts, axis=-1)
        elif c0 >= 2 * RET_QK_WIDTH + RET_V_WIDTH:
            r = r * jax.nn.sigmoid(r)
        o_ref[:, c0:c0 + PROJ_CW] = r.astype(BF16)


def _ret_inproj(x2d, col, nw, w, cos, sin, seq):
    n = x2d.shape[0]
    width = w.shape[1]
    tpb = seq // PROJ_TM
    return pl.pallas_call(
        _ret_inproj_kernel,
        grid=(n // PROJ_TM,),
        in_specs=[
            pl.BlockSpec((PROJ_TM, D_MODEL), lambda i: (i, col)),
            pl.BlockSpec((1, D_MODEL), lambda i: (0, 0)),
            pl.BlockSpec((D_MODEL, width), lambda i: (0, 0)),
            pl.BlockSpec((PROJ_TM, RET_DK // 2), lambda i: (i % tpb, 0)),
            pl.BlockSpec((PROJ_TM, RET_DK // 2), lambda i: (i % tpb, 0)),
        ],
        out_specs=pl.BlockSpec((PROJ_TM, width), lambda i: (i, 0)),
        out_shape=jax.ShapeDtypeStruct((n, width), BF16),
        scratch_shapes=[pltpu.VMEM((PROJ_TM, D_MODEL), BF16)],
        compiler_params=_cparams(("arbitrary",)),
        name="ret_inproj",
    )(x2d, nw, w, cos, sin)


def _hgrn_inproj_kernel(xy_ref, nw_ref, w_ref, lb_ref, a_ref, l_ref, xn_scr):
    xn_scr[...] = _rms_rows(_load_tok(xy_ref, X_ROWS, PROJ_TM), nw_ref[...]).astype(BF16)
    width = w_ref.shape[1]
    for j in range(width // PROJ_CW):
        c0 = j * PROJ_CW
        r = jnp.dot(xn_scr[...], w_ref[:, c0:c0 + PROJ_CW], preferred_element_type=F32)
        if c0 < D_MODEL:
            a_ref[:, c0:c0 + PROJ_CW] = (r * jax.nn.sigmoid(r)).astype(BF16)
        elif c0 < 2 * D_MODEL:
            a_ref[:, c0:c0 + PROJ_CW] = r.astype(BF16)
        elif c0 < 4 * D_MODEL:
            g0 = c0 - 2 * D_MODEL
            lb = lb_ref[:, g0:g0 + PROJ_CW]
            f = lb + (1.0 - lb) * jax.nn.sigmoid(r)
            a_ref[:, c0:c0 + PROJ_CW] = (1.0 - f).astype(BF16)
            l_ref[:, g0:g0 + PROJ_CW] = jnp.log(f)
        else:
            a_ref[:, c0:c0 + PROJ_CW] = jax.nn.sigmoid(r).astype(BF16)


def _hgrn_inproj(xy, nw, w, lb):
    n = xy.shape[0] // TOK_ROWS
    width = w.shape[1]
    return pl.pallas_call(
        _hgrn_inproj_kernel,
        grid=(n // PROJ_TM,),
        in_specs=[
            pl.BlockSpec((PROJ_TM * TOK_ROWS, LANE), lambda i: (i, 0)),
            pl.BlockSpec((1, D_MODEL), lambda i: (0, 0)),
            pl.BlockSpec((D_MODEL, width), lambda i: (0, 0)),
            pl.BlockSpec((1, 2 * D_MODEL), lambda i: (0, 0)),
        ],
        out_specs=[
            pl.BlockSpec((PROJ_TM, width), lambda i: (i, 0)),
            pl.BlockSpec((PROJ_TM, 2 * D_MODEL), lambda i: (i, 0)),
        ],
        out_shape=[
            jax.ShapeDtypeStruct((n, width), BF16),
            jax.ShapeDtypeStruct((n, 2 * D_MODEL), F32),
        ],
        scratch_shapes=[pltpu.VMEM((PROJ_TM, D_MODEL), BF16)],
        compiler_params=_cparams(("arbitrary",)),
        name="hgrn_inproj",
    )(xy, nw, w, lb)


_NT = (((1,), (1,)), ((), ()))
_TN = (((0,), (0,)), ((), ()))


def _ret_bwd_kernel(q_ref, k_ref, v_ref, ob_ref, r_scr, *, lg):
    @pl.when(pl.program_id(1) == 0)
    def _():
        r_scr[...] = jnp.zeros_like(r_scr)

    c = q_ref.shape[0]
    pos = lax.broadcasted_iota(jnp.int32, (c, 1), 0).astype(F32)
    for h in range(RET_HEADS):
        qh = q_ref[:, h * RET_DK:(h + 1) * RET_DK].astype(F32)
        kh = k_ref[:, h * RET_DK:(h + 1) * RET_DK].astype(F32)
        vh = v_ref[:, h * RET_DV:(h + 1) * RET_DV]
        qd = (qh * jnp.exp((c - pos) * lg[h])).astype(BF16)
        o = jnp.dot(qd, r_scr[h].astype(BF16), preferred_element_type=F32)
        ob_ref[:, h * RET_DV:(h + 1) * RET_DV] = o.astype(BF16)
        kd = (kh * jnp.exp(pos * lg[h])).astype(BF16)
        r_scr[h] = r_scr[h] * math.exp(c * lg[h]) + lax.dot_general(
            kd, vh, _TN, preferred_element_type=F32)


def _ret_fwd_kernel(q_ref, k_ref, v_ref, g_ref, ob_ref, x_ref, m_ref, wo_ref, xy_ref,
                    r_scr, *, lg):
    @pl.when(pl.program_id(1) == 0)
    def _():
        r_scr[...] = jnp.zeros_like(r_scr)

    c = q_ref.shape[0]
    pos = lax.broadcasted_iota(jnp.int32, (c, 1), 0).astype(F32)
    acc = x_ref[...]
    for h in range(RET_HEADS):
        qb = q_ref[:, h * RET_DK:(h + 1) * RET_DK]
        kb = k_ref[:, h * RET_DK:(h + 1) * RET_DK]
        vh = v_ref[:, h * RET_DV:(h + 1) * RET_DV]
        s = lax.dot_general(qb, kb, _NT, preferred_element_type=F32)
        sm = (s * m_ref[h]).astype(BF16)
        qd = (qb.astype(F32) * jnp.exp((pos + 1.0) * lg[h])).astype(BF16)
        o = (jnp.dot(sm, vh, preferred_element_type=F32)
             + jnp.dot(qd, r_scr[h].astype(BF16), preferred_element_type=F32)
             + ob_ref[:, h * RET_DV:(h + 1) * RET_DV].astype(F32))
        kd = (kb.astype(F32) * jnp.exp((c - 1.0 - pos) * lg[h])).astype(BF16)
        r_scr[h] = r_scr[h] * math.exp(c * lg[h]) + lax.dot_general(
            kd, vh, _TN, preferred_element_type=F32)
        ms = jnp.mean(o * o, axis=-1, keepdims=True)
        gated = (o * lax.rsqrt(ms + NORM_EPS)
                 * g_ref[:, h * RET_DV:(h + 1) * RET_DV].astype(F32)).astype(BF16)
        acc = acc + jnp.dot(gated, wo_ref[h * RET_DV:(h + 1) * RET_DV, :],
                            preferred_element_type=F32)
    _store_tok(xy_ref, 0, acc)
    _store_tok(xy_ref, X_ROWS, acc)


def _retention(proj, x3, decay_mat, w_out):
    b, t, _ = proj.shape
    c = RET_BLOCK
    nc = t // c
    lgf = _ret_log_gamma(RET_DECAY_FWD)
    lgb = _ret_log_gamma(RET_DECAY_BWD)
    qk = RET_QK_WIDTH
    vw = RET_V_WIDTH
    state = pltpu.VMEM((RET_HEADS, RET_DK, RET_DV), F32)

    ob = pl.pallas_call(
        functools.partial(_ret_bwd_kernel, lg=lgb),
        grid=(b, nc),
        in_specs=[
            pl.BlockSpec((None, c, qk), lambda bi, i: (bi, nc - 1 - i, 0)),
            pl.BlockSpec((None, c, qk), lambda bi, i: (bi, nc - 1 - i, 1)),
            pl.BlockSpec((None, c, vw), lambda bi, i: (bi, nc - 1 - i, 1)),
        ],
        out_specs=pl.BlockSpec((None, c, vw), lambda bi, i: (bi, nc - 1 - i, 0)),
        out_shape=jax.ShapeDtypeStruct((b, t, vw), BF16),
        scratch_shapes=[state],
        compiler_params=_cparams(("arbitrary", "arbitrary")),
        name="ret_bwd",
    )(proj, proj, proj)

    return pl.pallas_call(
        functools.partial(_ret_fwd_kernel, lg=lgf),
        grid=(b, nc),
        in_specs=[
            pl.BlockSpec((None, c, qk), lambda bi, i: (bi, i, 0)),
            pl.BlockSpec((None, c, qk), lambda bi, i: (bi, i, 1)),
            pl.BlockSpec((None, c, vw), lambda bi, i: (bi, i, 1)),
            pl.BlockSpec((None, c, vw), lambda bi, i: (bi, i, 2)),
            pl.BlockSpec((None, c, vw), lambda bi, i: (bi, i, 0)),
            pl.BlockSpec((None, c, D_MODEL), lambda bi, i: (bi, i, 0)),
            pl.BlockSpec((RET_HEADS, c, c), lambda bi, i: (0, 0, 0)),
            pl.BlockSpec((vw, D_MODEL), lambda bi, i: (0, 0)),
        ],
        out_specs=pl.BlockSpec((None, c * TOK_ROWS, LANE), lambda bi, i: (bi, i, 0)),
        out_shape=jax.ShapeDtypeStruct((b, t * TOK_ROWS, LANE), F32),
        scratch_shapes=[state],
        compiler_params=_cparams(("arbitrary", "arbitrary")),
        name="ret_fwd",
    )(proj, proj, proj, proj, ob, x3, decay_mat, w_out)


def _ret_decay_matrix():
    c = RET_BLOCK
    rel = (jnp.arange(c, dtype=F32)[:, None] - jnp.arange(c, dtype=F32)[None, :])[None]
    lgf = jnp.asarray(_ret_log_gamma(RET_DECAY_FWD), F32)[:, None, None]
    lgb = jnp.asarray(_ret_log_gamma(RET_DECAY_BWD), F32)[:, None, None]
    return jnp.where(rel >= 0, jnp.exp(jnp.maximum(rel, 0.0) * lgf),
                     jnp.exp(jnp.maximum(-rel, 0.0) * lgb))


def _tri_cumsum(tri, x):
    hi = x.astype(BF16)
    r1 = x - hi.astype(F32)
    mid = r1.astype(BF16)
    lo = (r1 - mid.astype(F32)).astype(BF16)
    return (jnp.dot(tri, hi, preferred_element_type=F32)
            + jnp.dot(tri, mid, preferred_element_type=F32)
            + jnp.dot(tri, lo, preferred_element_type=F32))


def _hgrn_chunk(q, k, v, lf, s_scr, reverse):
    c = q.shape[0]
    row = lax.broadcasted_iota(jnp.int32, (c, c), 0)
    col = lax.broadcasted_iota(jnp.int32, (c, c), 1)
    keep = (col >= row) if reverse else (col <= row)
    g = _tri_cumsum(keep.astype(BF16), lf)
    g_last = g[0:1, :] if reverse else g[c - 1:c, :]
    qg = (q.astype(F32) * jnp.exp(g)).astype(BF16)
    kf = k.astype(F32)
    kg = (kf * jnp.exp(-g)).astype(BF16)
    kd = (kf * jnp.exp(g_last - g)).astype(BF16)
    e_last = jnp.exp(g_last)
    outs = []
    for h in range(HGRN_HEADS):
        cs = slice(h * HGRN_DK, (h + 1) * HGRN_DK)
        s = lax.dot_general(qg[:, cs], kg[:, cs], _NT, preferred_element_type=F32)
        s = jnp.where(keep, s, 0.0).astype(BF16)
        st = s_scr[h]
        o = (jnp.dot(s, v[:, cs], preferred_element_type=F32)
             + lax.dot_general(qg[:, cs], st.astype(BF16), _NT, preferred_element_type=F32))
        s_scr[h] = st * e_last[:, cs] + lax.dot_general(
            v[:, cs], kd[:, cs], _TN, preferred_element_type=F32)
        outs.append(o)
    return jnp.concatenate(outs, axis=-1)


def _hgrn_bwd_kernel(q_ref, v_ref, k_ref, lf_ref, ob_ref, s_scr):
    @pl.when(pl.program_id(1) == 0)
    def _():
        s_scr[...] = jnp.zeros_like(s_scr)

    for sc in reversed(range(HGRN_BLOCK // HGRN_CHUNK)):
        rows = slice(sc * HGRN_CHUNK, (sc + 1) * HGRN_CHUNK)
        o = _hgrn_chunk(q_ref[rows, :], k_ref[rows, :], v_ref[rows, :], lf_ref[rows, :],
                        s_scr, reverse=True)
        ob_ref[rows, :] = o.astype(BF16)


def _hgrn_fwd_kernel(q_ref, v_ref, k_ref, lf_ref, sg_ref, ob_ref, x_ref, nw_ref, wo_ref,
                     xy_ref, s_scr, o_scr):
    @pl.when(pl.program_id(1) == 0)
    def _():
        s_scr[...] = jnp.zeros_like(s_scr)

    for sc in range(HGRN_BLOCK // HGRN_CHUNK):
        rows = slice(sc * HGRN_CHUNK, (sc + 1) * HGRN_CHUNK)
        o = _hgrn_chunk(q_ref[rows, :], k_ref[rows, :], v_ref[rows, :], lf_ref[rows, :],
                        s_scr, reverse=False)
        o_scr[rows, :] = o + ob_ref[rows, :].astype(F32)
    on = _rms_rows(o_scr[...], nw_ref[...])
    gated = (on * sg_ref[...].astype(F32)).astype(BF16)
    xn = (_load_tok(x_ref, X_ROWS, HGRN_BLOCK)
          + jnp.dot(gated, wo_ref[...], preferred_element_type=F32))
    _store_tok(xy_ref, 0, xn)
    _store_tok(xy_ref, X_ROWS, xn)


def _hgrn(a3, l3, xy3, norm_w, w_out):
    b, t, _ = a3.shape
    tb = HGRN_BLOCK
    nb = t // tb
    d = D_MODEL
    state = pltpu.VMEM((HGRN_HEADS, HGRN_DK, HGRN_DK), F32)

    ob = pl.pallas_call(
        _hgrn_bwd_kernel,
        grid=(b, nb),
        in_specs=[
            pl.BlockSpec((None, tb, d), lambda bi, i: (bi, nb - 1 - i, 0)),
            pl.BlockSpec((None, tb, d), lambda bi, i: (bi, nb - 1 - i, 1)),
            pl.BlockSpec((None, tb, d), lambda bi, i: (bi, nb - 1 - i, 3)),
            pl.BlockSpec((None, tb, d), lambda bi, i: (bi, nb - 1 - i, 1)),
        ],
        out_specs=pl.BlockSpec((None, tb, d), lambda bi, i: (bi, nb - 1 - i, 0)),
        out_shape=jax.ShapeDtypeStruct((b, t, d), BF16),
        scratch_shapes=[state],
        compiler_params=_cparams(("arbitrary", "arbitrary")),
        name="hgrn_bwd",
    )(a3, a3, a3, l3)

    return pl.pallas_call(
        _hgrn_fwd_kernel,
        grid=(b, nb),
        in_specs=[
            pl.BlockSpec((None, tb, d), lambda bi, i: (bi, i, 0)),
            pl.BlockSpec((None, tb, d), lambda bi, i: (bi, i, 1)),
            pl.BlockSpec((None, tb, d), lambda bi, i: (bi, i, 2)),
            pl.BlockSpec((None, tb, d), lambda bi, i: (bi, i, 0)),
            pl.BlockSpec((None, tb, d), lambda bi, i: (bi, i, 4)),
            pl.BlockSpec((None, tb, d), lambda bi, i: (bi, i, 0)),
            pl.BlockSpec((None, tb * TOK_ROWS, LANE), lambda bi, i: (bi, i, 0)),
            pl.BlockSpec((1, d), lambda bi, i: (0, 0)),
            pl.BlockSpec((d, d), lambda bi, i: (0, 0)),
        ],
        out_specs=pl.BlockSpec((None, tb * TOK_ROWS, LANE), lambda bi, i: (bi, i, 0)),
        out_shape=jax.ShapeDtypeStruct((b, t * TOK_ROWS, LANE), F32),
        scratch_shapes=[state, pltpu.VMEM((tb, d), F32)],
        compiler_params=_cparams(("arbitrary", "arbitrary")),
        name="hgrn_fwd",
    )(a3, a3, a3, l3, a3, ob, xy3, norm_w, w_out)


def _router_kernel(xy_ref, nw_ref, wr_ref, aff_ref):
    xn = _rms_rows(_load_tok(xy_ref, 0, PROJ_TM), nw_ref[...])
    xh = xn.astype(BF16)
    xl = (xn - xh.astype(F32)).astype(BF16)
    wr = wr_ref[...]
    wh = wr.astype(BF16)
    wl = (wr - wh.astype(F32)).astype(BF16)
    logits = (lax.dot_general(wh, xh, _NT, preferred_element_type=F32)
              + lax.dot_general(wh, xl, _NT, preferred_element_type=F32)
              + lax.dot_general(wl, xh, _NT, preferred_element_type=F32))
    m = jnp.max(logits, axis=0, keepdims=True)
    e = jnp.exp(logits - m)
    aff_ref[...] = e / jnp.sum(e, axis=0, keepdims=True)


def _router(xy, nw, wr_t):
    n = xy.shape[0] // TOK_ROWS
    return pl.pallas_call(
        _router_kernel,
        grid=(n // PROJ_TM,),
        in_specs=[
            pl.BlockSpec((PROJ_TM * TOK_ROWS, LANE), lambda i: (i, 0)),
            pl.BlockSpec((1, D_MODEL), lambda i: (0, 0)),
            pl.BlockSpec((N_EXPERTS, D_MODEL), lambda i: (0, 0)),
        ],
        out_specs=pl.BlockSpec((N_EXPERTS, PROJ_TM), lambda i: (0, i)),
        out_shape=jax.ShapeDtypeStruct((N_EXPERTS, n), F32),
        compiler_params=_cparams(("arbitrary",)),
        name="router",
    )(xy, nw, wr_t)


def _ffn_kernel(idx_ref, idxf_ref, idxp_ref, idxn_ref, gate_ref, nw_ref, wg_ref, wu_ref, wd_ref,
                xy_in_ref, xy_ref, buf, work, xn_scr, sem_g, sem_s, *, tiles_per_expert):
    del xy_in_ref
    tm = FFN_TM
    pitch = FFN_PITCH
    e = pl.program_id(0)
    i = pl.program_id(1)
    step = e * tiles_per_expert + i
    n_steps = pl.num_programs(0) * tiles_per_expert
    slot = step % 2
    last_tile = i == tiles_per_expert - 1
    base = pl.multiple_of(slot * (tm * pitch), 8)
    st_cur = pl.multiple_of(slot * (tm * X_ROWS), 8)
    st_prev = pl.multiple_of((1 - slot) * (tm * X_ROWS), 8)
    acc0 = 2 * tm * X_ROWS

    def start_gather(ids, dst_slot, lo=0, hi=FFN_TM):
        dst = pl.multiple_of(dst_slot * (tm * pitch), 8)
        for r in range(lo, hi):
            t = pl.multiple_of(ids[0, r] * TOK_ROWS, TOK_ROWS)
            pltpu.make_async_copy(xy_ref.at[pl.ds(t, TOK_ROWS), :],
                                  buf.at[pl.ds(dst + r * pitch, TOK_ROWS), :],
                                  sem_g.at[dst_slot]).start()

    def wait_gather(dst_slot):
        pltpu.make_async_copy(xy_ref.at[pl.ds(0, tm * TOK_ROWS), :],
                              buf.at[pl.ds(0, tm * TOK_ROWS), :], sem_g.at[dst_slot]).wait()

    def start_scatter(ids, src, lo=0, hi=FFN_TM):
        for r in range(lo, hi):
            t = pl.multiple_of(ids[0, r] * TOK_ROWS + X_ROWS, X_ROWS)
            pltpu.make_async_copy(work.at[pl.ds(src + r * X_ROWS, X_ROWS), :],
                                  xy_ref.at[pl.ds(t, X_ROWS), :], sem_s.at[0]).start()

    def wait_scatter():
        pltpu.make_async_copy(work.at[pl.ds(0, tm * X_ROWS), :],
                              xy_ref.at[pl.ds(0, tm * X_ROWS), :], sem_s.at[0]).wait()

    @pl.when(step == 0)
    def _():
        start_gather(idx_ref, 0)

    wait_gather(slot)

    @pl.when(i == 0)
    def _():
        for j in range(X_ROWS):
            work[pl.ds(st_prev + j, tm, stride=X_ROWS), :] = (
                buf[pl.ds(base + X_ROWS + j, tm, stride=pitch), :])

    xn_scr[...] = _rms_rows(_load_tok(buf, base, tm, pitch), nw_ref[...]).astype(BF16)

    n_f = EXPERT_FF // FFN_FW
    n_pts = 3 * n_f
    cuts = [(tm * k) // n_pts for k in range(n_pts + 1)]

    def issue(k):
        start_scatter(idxp_ref, st_prev, cuts[k], cuts[k + 1])
        start_gather(idxf_ref, 1 - slot, cuts[k], cuts[k + 1])

    eye = (lax.broadcasted_iota(jnp.int32, (LANE, LANE), 0)
           == lax.broadcasted_iota(jnp.int32, (LANE, LANE), 1)).astype(F32)
    gate_col = jnp.concatenate(
        [jnp.sum(eye * gate_ref[r:r + 1, :], axis=1, keepdims=True) for r in range(tm // LANE)],
        axis=0)
    for f in range(n_f):
        fs = slice(f * FFN_FW, (f + 1) * FFN_FW)
        issue(3 * f)
        g = jnp.dot(xn_scr[...], wg_ref[:, fs], preferred_element_type=F32)
        issue(3 * f + 1)
        u = jnp.dot(xn_scr[...], wu_ref[:, fs], preferred_element_type=F32)
        hcur = (g * jax.nn.sigmoid(g) * u * gate_col).astype(BF16)
        issue(3 * f + 2)
        part = jnp.dot(hcur, wd_ref[fs, :], preferred_element_type=F32)
        if f < n_f - 1:
            for j in range(X_ROWS):
                rows = pl.ds(acc0 + j * tm, tm)
                cs = slice(j * LANE, (j + 1) * LANE)
                if f == 0:
                    work[rows, :] = part[:, cs]
                elif f == 1:
                    work[rows, :] = (work[rows, :] + part[:, cs]
                                     + buf[pl.ds(base + X_ROWS + j, tm, stride=pitch), :])
                else:
                    work[rows, :] += part[:, cs]
        else:
            wait_scatter()
            for j in range(X_ROWS):
                cs = slice(j * LANE, (j + 1) * LANE)
                work[pl.ds(st_cur + j, tm, stride=X_ROWS), :] = (
                    work[pl.ds(acc0 + j * tm, tm), :] + part[:, cs])

    @pl.when(last_tile)
    def _():
        wait_gather(1 - slot)
        start_scatter(idx_ref, st_cur)
        wait_scatter()

        @pl.when(step < n_steps - 1)
        def _():
            start_gather(idxn_ref, 1 - slot)


def _expert_ffn(xy, idx, gate, nw, wg, wu, wd):
    cap = idx.shape[1]
    tm = FFN_TM
    nt = cap // tm
    assert nt >= 2 and EXPERT_FF // FFN_FW >= 3
    n_steps = N_EXPERTS * nt
    idx3 = idx.reshape(n_steps, 1, tm)
    gate3 = gate.reshape(n_steps, tm // LANE, LANE)
    smem_tile = functools.partial(pl.BlockSpec, (None, 1, tm), memory_space=pltpu.SMEM)
    return pl.pallas_call(
        functools.partial(_ffn_kernel, tiles_per_expert=nt),
        grid=(N_EXPERTS, nt),
        in_specs=[
            smem_tile(lambda e, i: (e * nt + i, 0, 0)),
            smem_tile(lambda e, i: (e * nt + jnp.minimum(i + 1, nt - 1), 0, 0)),
            smem_tile(lambda e, i: (e * nt + jnp.maximum(i - 1, 0), 0, 0)),
            smem_tile(lambda e, i: (jnp.minimum(e * nt + i + 1, n_steps - 1), 0, 0)),
            pl.BlockSpec((None, tm // LANE, LANE), lambda e, i: (e * nt + i, 0, 0)),
            pl.BlockSpec((1, D_MODEL), lambda e, i: (0, 0)),
            pl.BlockSpec((None, D_MODEL, EXPERT_FF), lambda e, i: (e, 0, 0)),
            pl.BlockSpec((None, D_MODEL, EXPERT_FF), lambda e, i: (e, 0, 0)),
            pl.BlockSpec((None, EXPERT_FF, D_MODEL), lambda e, i: (e, 0, 0)),
            pl.BlockSpec(memory_space=pl.ANY),
        ],
        out_specs=pl.BlockSpec(memory_space=pl.ANY),
        out_shape=jax.ShapeDtypeStruct(xy.shape, F32),
        scratch_shapes=[
            pltpu.VMEM((2 * tm * FFN_PITCH, LANE), F32),
            pltpu.VMEM((3 * tm * X_ROWS, LANE), F32),
            pltpu.VMEM((tm, D_MODEL), BF16),
            pltpu.SemaphoreType.DMA((2,)),
            pltpu.SemaphoreType.DMA((1,)),
        ],
        input_output_aliases={9: 0},
        compiler_params=_cparams(("arbitrary", "arbitrary")),
        name="expert_ffn",
    )(idx3, idx3, idx3, idx3, gate3, nw, wg, wu, wd, xy)


def _threshold_kernel(aff_ref, thr_ref, need_ref, *, k):
    bits = lax.bitcast_convert_type(aff_ref[...], I32)

    def body(b, t):
        cand = t | (jnp.int32(1) << (30 - b))
        cnt = jnp.sum((bits >= cand).astype(I32), axis=1, keepdims=True)
        return jnp.where(cnt >= k, cand, t)

    t = lax.fori_loop(0, 31, body, jnp.zeros((N_EXPERTS, 1), I32))
    n_gt = jnp.sum((bits > t).astype(I32), axis=1, keepdims=True)
    thr_ref[...] = jnp.broadcast_to(lax.bitcast_convert_type(t, F32), thr_ref.shape)
    need_ref[...] = jnp.broadcast_to(k - n_gt, need_ref.shape)


def _threshold(aff, k):
    return pl.pallas_call(
        functools.partial(_threshold_kernel, k=k),
        out_shape=[jax.ShapeDtypeStruct((N_EXPERTS, LANE), F32),
                   jax.ShapeDtypeStruct((N_EXPERTS, LANE), I32)],
        compiler_params=pltpu.CompilerParams(vmem_limit_bytes=VMEM_LIMIT_BYTES),
        name="topk_threshold",
    )(aff)


def _compact_sc(aff_flat, thr_flat, need_flat, n, cap):
    lanes = SC_LANES
    mesh = plsc.VectorSubcoreMesh(core_axis_name="c", subcore_axis_name="s", num_cores=1,
                                  num_subcores=N_EXPERTS)

    @functools.partial(
        pl.kernel,
        out_type=(jax.ShapeDtypeStruct((N_EXPERTS * cap,), I32),
                  jax.ShapeDtypeStruct((N_EXPERTS * cap,), F32)),
        mesh=mesh,
        scratch_types=[pltpu.VMEM((n,), F32), pltpu.VMEM((cap,), I32), pltpu.VMEM((cap,), F32),
                       pltpu.VMEM((lanes,), F32), pltpu.VMEM((lanes,), I32)],
        compiler_params=pltpu.CompilerParams(needs_layout_passes=False),
        name="topk_compact",
    )
    def compact(aff_hbm, thr_hbm, need_hbm, idx_hbm, gate_hbm, row, idx_v, gate_v, thr_v, need_v):
        e = lax.axis_index("s")
        pltpu.sync_copy(aff_hbm.at[pl.ds(e * n, n)], row)
        pltpu.sync_copy(thr_hbm.at[pl.ds(e * LANE, lanes)], thr_v)
        pltpu.sync_copy(need_hbm.at[pl.ds(e * LANE, lanes)], need_v)
        thr = thr_v[...]
        need = need_v[...]
        lane = lax.iota(I32, lanes)
        zero_i = jnp.zeros((lanes,), I32)

        def init(i, carry):
            idx_v[pl.ds(i * lanes, lanes)] = zero_i
            gate_v[pl.ds(i * lanes, lanes)] = jnp.zeros((lanes,), F32)
            return carry

        lax.fori_loop(0, cap // lanes, init, 0)

        def body(i, carry):
            off, eqs = carry
            v = row[pl.ds(i * lanes, lanes)]
            m_gt = v > thr
            m_eq = v == thr
            eq_rank = plsc.cumsum(m_eq.astype(I32)) + eqs
            take = m_gt | (m_eq & (eq_rank <= need))
            pos = off + plsc.cumsum(take.astype(I32)) - 1
            take = take & (pos < cap)
            plsc.store_scatter(idx_v, [pos], lane + i * lanes, mask=take)
            plsc.store_scatter(gate_v, [pos], v, mask=take)
            off = off + plsc.all_reduce_population_count(take)
            eqs = eqs + plsc.all_reduce_population_count(m_eq)
            return off, eqs

        lax.fori_loop(0, n // lanes, body, (zero_i, zero_i))
        pltpu.sync_copy(idx_v, idx_hbm.at[pl.ds(e * cap, cap)])
        pltpu.sync_copy(gate_v, gate_hbm.at[pl.ds(e * cap, cap)])

    return compact(aff_flat, thr_flat, need_flat)


def _expert_choice(aff, cap):
    n = aff.shape[1]
    thr, need = _threshold(aff, cap)
    idx, gate = _compact_sc(aff.reshape(-1), thr.reshape(-1), need.reshape(-1), n, cap)
    return gate.reshape(N_EXPERTS, cap), idx.reshape(N_EXPERTS, cap)


def _moe(xy, nw, wr_t, wg, wu, wd):
    n = xy.shape[0] // TOK_ROWS
    cap = EC_CAPACITY_FACTOR * n // N_EXPERTS
    aff = _router(xy, nw, wr_t)
    gate, idx = _expert_choice(aff, cap)
    return _expert_ffn(xy, idx, gate, nw, wg, wu, wd)


def _final_norm_kernel(xy_ref, nw_ref, o_ref):
    o_ref[...] = _rms_rows(_load_tok(xy_ref, X_ROWS, PROJ_TM), nw_ref[...])


def _final_norm(xy, nw):
    n = xy.shape[0] // TOK_ROWS
    return pl.pallas_call(
        _final_norm_kernel,
        grid=(n // PROJ_TM,),
        in_specs=[pl.BlockSpec((PROJ_TM * TOK_ROWS, LANE), lambda i: (i, 0)),
                  pl.BlockSpec((1, D_MODEL), lambda i: (0, 0))],
        out_specs=pl.BlockSpec((PROJ_TM, D_MODEL), lambda i: (i, 0)),
        out_shape=jax.ShapeDtypeStruct((n, D_MODEL), F32),
        compiler_params=_cparams(("arbitrary",)),
        name="final_norm",
    )(xy, nw)


def _rope_tables(seq):
    half = RET_DK // 2
    inv = 1.0 / (ROPE_BASE ** (jnp.arange(half, dtype=F32) / half))
    ang = jnp.arange(seq, dtype=F32)[:, None] * inv[None, :]
    return jnp.cos(ang), jnp.sin(ang)


def _trunk(x, p):
    b, t, d = x.shape
    n = b * t
    cos, sin = _rope_tables(t)

    proj = _ret_inproj(x.reshape(n, d), 0, p["norm_mix"][0], p["ret_w_in"], cos, sin, t)
    xy = _retention(proj.reshape(b, t, -1), x, p["ret_decay"], p["ret_w_out"])
    xy = _moe(xy.reshape(n * TOK_ROWS, LANE), p["norm_ffn"][0], p["w_router_t"][0],
              p["w_gate"][0], p["w_up"][0], p["w_down"][0])

    a, lf = _hgrn_inproj(xy, p["norm_mix"][1], p["hgrn_w_in"], p["hgrn_lb"])
    xy = _hgrn(a.reshape(b, t, -1), lf.reshape(b, t, -1), xy.reshape(b, t * TOK_ROWS, LANE),
               p["hgrn_norm"], p["hgrn_w_out"])
    xy = _moe(xy.reshape(n * TOK_ROWS, LANE), p["norm_ffn"][1], p["w_router_t"][1],
              p["w_gate"][1], p["w_up"][1], p["w_down"][1])

    return _final_norm(xy, p["norm_final"]).reshape(b, t, d)


def kernel(x_prompt, x_sample, norm_mix_w, norm_ffn_w, norm_final_w, ret_w_in, ret_w_out,
           hgrn_w_in, hgrn_lb, hgrn_norm_w, hgrn_w_out, moe_w_router, moe_w_gate, moe_w_up,
           moe_w_down):
    depth = norm_mix_w.shape[0]
    assert depth == 2 and ret_w_in.shape[0] == 1 and hgrn_w_in.shape[0] == 1
    sm = jax.nn.softmax(hgrn_lb.astype(F32), axis=0)
    lower_bounds = jnp.cumsum(sm, axis=0) - sm[0:1]
    p = {
        "norm_mix": norm_mix_w.reshape(depth, 1, D_MODEL),
        "norm_ffn": norm_ffn_w.reshape(depth, 1, D_MODEL),
        "norm_final": norm_final_w.reshape(1, D_MODEL),
        "ret_w_in": ret_w_in[0].astype(BF16),
        "ret_w_out": ret_w_out[0].astype(BF16),
        "ret_decay": _ret_decay_matrix(),
        "hgrn_w_in": hgrn_w_in[0].astype(BF16),
        "hgrn_lb": lower_bounds[1].reshape(1, 2 * D_MODEL),
        "hgrn_norm": hgrn_norm_w[0].reshape(1, D_MODEL),
        "hgrn_w_out": hgrn_w_out[0].astype(BF16),
        "w_router_t": jnp.swapaxes(moe_w_router, 1, 2),
        "w_gate": moe_w_gate.astype(BF16),
        "w_up": moe_w_up.astype(BF16),
        "w_down": moe_w_down.astype(BF16),
    }
    return _trunk(x_prompt, p), _trunk(x_sample, p)
```

```python
import functools
import math

import jax
import jax.numpy as jnp
from jax import lax
from jax.experimental import pallas as pl
from jax.experimental.pallas import tpu as pltpu
from jax.experimental.pallas import tpu_sc as plsc

F32 = jnp.float32
I32 = jnp.int32
BF16 = jnp.bfloat16

D_MODEL = 1024
NORM_EPS = 1e-6
ROPE_BASE = 10000.0

RET_HEADS = 4
RET_DK = 256
RET_DV = 512
RET_QK_WIDTH = RET_HEADS * RET_DK
RET_V_WIDTH = RET_HEADS * RET_DV
RET_DECAY_FWD = 5.0
RET_DECAY_BWD = 5.5
RET_BLOCK = 256

HGRN_HEADS = 8
HGRN_DK = 128
HGRN_CHUNK = 64
HGRN_BLOCK = 256

N_EXPERTS = 16
EC_CAPACITY_FACTOR = 2
EXPERT_FF = 2 * D_MODEL

PROJ_TM = 512
PROJ_CW = 512
FFN_TM = 512
FFN_FW = 512
LANE = 128
SC_LANES = 16
X_ROWS = D_MODEL // LANE
TOK_ROWS = 2 * X_ROWS
FFN_PITCH = 24

VMEM_LIMIT_BYTES = 52 * 1024 * 1024


def _cparams(sem):
    return pltpu.CompilerParams(dimension_semantics=sem, vmem_limit_bytes=VMEM_LIMIT_BYTES)


def _ret_log_gamma(offset):
    return [math.log1p(-(2.0 ** (-offset - h))) for h in range(RET_HEADS)]


def _rms_rows(x, w):
    ms = jnp.mean(x * x, axis=-1, keepdims=True)
    return x * lax.rsqrt(ms + NORM_EPS) * w


def _load_tok(ref, first_row, n_tok, pitch=TOK_ROWS):
    return jnp.concatenate(
        [ref[pl.ds(first_row + j, n_tok, stride=pitch), :] for j in range(X_ROWS)], axis=-1)


def _store_tok(ref, first_row, val, pitch=TOK_ROWS):
    for j in range(X_ROWS):
        ref[pl.ds(first_row + j, val.shape[0], stride=pitch), :] = val[:, j * LANE:(j + 1) * LANE]


def _ret_inproj_kernel(x_ref, nw_ref, w_ref, cos_ref, sin_ref, o_ref, xn_scr):
    xn_scr[...] = _rms_rows(x_ref[...], nw_ref[...]).astype(BF16)
    cos = cos_ref[...]
    sin = sin_ref[...]
    width = w_ref.shape[1]
    half = RET_DK // 2
    for j in range(width // PROJ_CW):
        c0 = j * PROJ_CW
        r = jnp.dot(xn_scr[...], w_ref[:, c0:c0 + PROJ_CW], preferred_element_type=F32)
        if c0 < 2 * RET_QK_WIDTH:
            scale = 1.0 if c0 < RET_QK_WIDTH else RET_DK ** -0.5
            parts = []
            for hh in range(PROJ_CW // RET_DK):
                x1 = r[:, hh * RET_DK:hh * RET_DK + half]
                x2 = r[:, hh * RET_DK + half:(hh + 1) * RET_DK]
                parts.append((x1 * cos - x2 * sin) * scale)
                parts.append((x1 * sin + x2 * cos) * scale)
            r = jnp.concatenate(parts, axis=-1)
        elif c0 >= 2 * RET_QK_WIDTH + RET_V_WIDTH:
            r = r * jax.nn.sigmoid(r)
        o_ref[:, c0:c0 + PROJ_CW] = r.astype(BF16)


def _ret_inproj(x2d, col, nw, w, cos, sin, seq):
    n = x2d.shape[0]
    width = w.shape[1]
    tpb = seq // PROJ_TM
    return pl.pallas_call(
        _ret_inproj_kernel,
        grid=(n // PROJ_TM,),
        in_specs=[
            pl.BlockSpec((PROJ_TM, D_MODEL), lambda i: (i, col)),
            pl.BlockSpec((1, D_MODEL), lambda i: (0, 0)),
            pl.BlockSpec((D_MODEL, width), lambda i: (0, 0)),
            pl.BlockSpec((PROJ_TM, RET_DK // 2), lambda i: (i % tpb, 0)),
            pl.BlockSpec((PROJ_TM, RET_DK // 2), lambda i: (i % tpb, 0)),
        ],
        out_specs=pl.BlockSpec((PROJ_TM, width), lambda i: (i, 0)),
        out_shape=jax.ShapeDtypeStruct((n, width), BF16),
        scratch_shapes=[pltpu.VMEM((PROJ_TM, D_MODEL), BF16)],
        compiler_params=_cparams(("arbitrary",)),
        name="ret_inproj",
    )(x2d, nw, w, cos, sin)


def _hgrn_inproj_kernel(xy_ref, nw_ref, w_ref, lb_ref, a_ref, l_ref, xn_scr):
    xn_scr[...] = _rms_rows(_load_tok(xy_ref, X_ROWS, PROJ_TM), nw_ref[...]).astype(BF16)
    width = w_ref.shape[1]
    for j in range(width // PROJ_CW):
        c0 = j * PROJ_CW
        r = jnp.dot(xn_scr[...], w_ref[:, c0:c0 + PROJ_CW], preferred_element_type=F32)
        if c0 < D_MODEL:
            a_ref[:, c0:c0 + PROJ_CW] = (r * jax.nn.sigmoid(r)).astype(BF16)
        elif c0 < 2 * D_MODEL:
            a_ref[:, c0:c0 + PROJ_CW] = r.astype(BF16)
        elif c0 < 4 * D_MODEL:
            g0 = c0 - 2 * D_MODEL
            lb = lb_ref[:, g0:g0 + PROJ_CW]
            f = lb + (1.0 - lb) * jax.nn.sigmoid(r)
            a_ref[:, c0:c0 + PROJ_CW] = (1.0 - f).astype(BF16)
            l_ref[:, g0:g0 + PROJ_CW] = jnp.log(f)
        else:
            a_ref[:, c0:c0 + PROJ_CW] = jax.nn.sigmoid(r).astype(BF16)


def _hgrn_inproj(xy, nw, w, lb):
    n = xy.shape[0] // TOK_ROWS
    width = w.shape[1]
    return pl.pallas_call(
        _hgrn_inproj_kernel,
        grid=(n // PROJ_TM,),
        in_specs=[
            pl.BlockSpec((PROJ_TM * TOK_ROWS, LANE), lambda i: (i, 0)),
            pl.BlockSpec((1, D_MODEL), lambda i: (0, 0)),
            pl.BlockSpec((D_MODEL, width), lambda i: (0, 0)),
            pl.BlockSpec((1, 2 * D_MODEL), lambda i: (0, 0)),
        ],
        out_specs=[
            pl.BlockSpec((PROJ_TM, width), lambda i: (i, 0)),
            pl.BlockSpec((PROJ_TM, 2 * D_MODEL), lambda i: (i, 0)),
        ],
        out_shape=[
            jax.ShapeDtypeStruct((n, width), BF16),
            jax.ShapeDtypeStruct((n, 2 * D_MODEL), F32),
        ],
        scratch_shapes=[pltpu.VMEM((PROJ_TM, D_MODEL), BF16)],
        compiler_params=_cparams(("arbitrary",)),
        name="hgrn_inproj",
    )(xy, nw, w, lb)


_NT = (((1,), (1,)), ((), ()))
_TN = (((0,), (0,)), ((), ()))


def _ret_bwd_kernel(q_ref, k_ref, v_ref, ob_ref, r_scr, *, lg):
    @pl.when(pl.program_id(1) == 0)
    def _():
        r_scr[...] = jnp.zeros_like(r_scr)

    c = q_ref.shape[0]
    pos = lax.broadcasted_iota(jnp.int32, (c, 1), 0).astype(F32)
    for h in range(RET_HEADS):
        qh = q_ref[:, h * RET_DK:(h + 1) * RET_DK].astype(F32)
        kh = k_ref[:, h * RET_DK:(h + 1) * RET_DK].astype(F32)
        vh = v_ref[:, h * RET_DV:(h + 1) * RET_DV]
        qd = (qh * jnp.exp((c - pos) * lg[h])).astype(BF16)
        o = jnp.dot(qd, r_scr[h].astype(BF16), preferred_element_type=F32)
        ob_ref[:, h * RET_DV:(h + 1) * RET_DV] = o.astype(BF16)
        kd = (kh * jnp.exp(pos * lg[h])).astype(BF16)
        r_scr[h] = r_scr[h] * math.exp(c * lg[h]) + lax.dot_general(
            kd, vh, _TN, preferred_element_type=F32)


def _ret_fwd_kernel(q_ref, k_ref, v_ref, g_ref, ob_ref, x_ref, m_ref, wo_ref, xy_ref,
                    r_scr, *, lg):
    @pl.when(pl.program_id(1) == 0)
    def _():
        r_scr[...] = jnp.zeros_like(r_scr)

    c = q_ref.shape[0]
    pos = lax.broadcasted_iota(jnp.int32, (c, 1), 0).astype(F32)
    acc = x_ref[...]
    for h in range(RET_HEADS):
        qb = q_ref[:, h * RET_DK:(h + 1) * RET_DK]
        kb = k_ref[:, h * RET_DK:(h + 1) * RET_DK]
        vh = v_ref[:, h * RET_DV:(h + 1) * RET_DV]
        s = lax.dot_general(qb, kb, _NT, preferred_element_type=F32)
        sm = (s * m_ref[h]).astype(BF16)
        qd = (qb.astype(F32) * jnp.exp((pos + 1.0) * lg[h])).astype(BF16)
        o = (jnp.dot(sm, vh, preferred_element_type=F32)
             + jnp.dot(qd, r_scr[h].astype(BF16), preferred_element_type=F32)
             + ob_ref[:, h * RET_DV:(h + 1) * RET_DV].astype(F32))
        kd = (kb.astype(F32) * jnp.exp((c - 1.0 - pos) * lg[h])).astype(BF16)
        r_scr[h] = r_scr[h] * math.exp(c * lg[h]) + lax.dot_general(
            kd, vh, _TN, preferred_element_type=F32)
        ms = jnp.mean(o * o, axis=-1, keepdims=True)
        gated = (o * lax.rsqrt(ms + NORM_EPS)
                 * g_ref[:, h * RET_DV:(h + 1) * RET_DV].astype(F32)).astype(BF16)
        acc = acc + jnp.dot(gated, wo_ref[h * RET_DV:(h + 1) * RET_DV, :],
                            preferred_element_type=F32)
    _store_tok(xy_ref, 0, acc)
    _store_tok(xy_ref, X_ROWS, acc)


def _retention(proj, x3, decay_mat, w_out):
    b, t, _ = proj.shape
    c = RET_BLOCK
    nc = t // c
    lgf = _ret_log_gamma(RET_DECAY_FWD)
    lgb = _ret_log_gamma(RET_DECAY_BWD)
    qk = RET_QK_WIDTH
    vw = RET_V_WIDTH
    state = pltpu.VMEM((RET_HEADS, RET_DK, RET_DV), F32)

    ob = pl.pallas_call(
        functools.partial(_ret_bwd_kernel, lg=lgb),
        grid=(b, nc),
        in_specs=[
            pl.BlockSpec((None, c, qk), lambda bi, i: (bi, nc - 1 - i, 0)),
            pl.BlockSpec((None, c, qk), lambda bi, i: (bi, nc - 1 - i, 1)),
            pl.BlockSpec((None, c, vw), lambda bi, i: (bi, nc - 1 - i, 1)),
        ],
        out_specs=pl.BlockSpec((None, c, vw), lambda bi, i: (bi, nc - 1 - i, 0)),
        out_shape=jax.ShapeDtypeStruct((b, t, vw), BF16),
        scratch_shapes=[state],
        compiler_params=_cparams(("arbitrary", "arbitrary")),
        name="ret_bwd",
    )(proj, proj, proj)

    return pl.pallas_call(
        functools.partial(_ret_fwd_kernel, lg=lgf),
        grid=(b, nc),
        in_specs=[
            pl.BlockSpec((None, c, qk), lambda bi, i: (bi, i, 0)),
            pl.BlockSpec((None, c, qk), lambda bi, i: (bi, i, 1)),
            pl.BlockSpec((None, c, vw), lambda bi, i: (bi, i, 1)),
            pl.BlockSpec((None, c, vw), lambda bi, i: (bi, i, 2)),
            pl.BlockSpec((None, c, vw), lambda bi, i: (bi, i, 0)),
            pl.BlockSpec((None, c, D_MODEL), lambda bi, i: (bi, i, 0)),
            pl.BlockSpec((RET_HEADS, c, c), lambda bi, i: (0, 0, 0)),
            pl.BlockSpec((vw, D_MODEL), lambda bi, i: (0, 0)),
        ],
        out_specs=pl.BlockSpec((None, c * TOK_ROWS, LANE), lambda bi, i: (bi, i, 0)),
        out_shape=jax.ShapeDtypeStruct((b, t * TOK_ROWS, LANE), F32),
        scratch_shapes=[state],
        compiler_params=_cparams(("arbitrary", "arbitrary")),
        name="ret_fwd",
    )(proj, proj, proj, proj, ob, x3, decay_mat, w_out)


def _ret_decay_matrix():
    c = RET_BLOCK
    rel = (jnp.arange(c, dtype=F32)[:, None] - jnp.arange(c, dtype=F32)[None, :])[None]
    lgf = jnp.asarray(_ret_log_gamma(RET_DECAY_FWD), F32)[:, None, None]
    lgb = jnp.asarray(_ret_log_gamma(RET_DECAY_BWD), F32)[:, None, None]
    return jnp.where(rel >= 0, jnp.exp(jnp.maximum(rel, 0.0) * lgf),
                     jnp.exp(jnp.maximum(-rel, 0.0) * lgb))


def _tri_cumsum(tri, x):
    hi = x.astype(BF16)
    lo = (x - hi.astype(F32)).astype(BF16)
    return (jnp.dot(tri, hi, preferred_element_type=F32)
            + jnp.dot(tri, lo, preferred_element_type=F32))


def _hgrn_block(q_ref, k_ref, v_ref, lf_ref, s_scr, reverse):
    tb = HGRN_BLOCK
    c = HGRN_CHUNK
    n_sub = tb // c
    shift = c.bit_length() - 1
    row = lax.broadcasted_iota(jnp.int32, (tb, tb), 0)
    col = lax.broadcasted_iota(jnp.int32, (tb, tb), 1)
    same_chunk = lax.shift_right_logical(row, shift) == lax.shift_right_logical(col, shift)
    keep = same_chunk & ((col >= row) if reverse else (col <= row))
    g = _tri_cumsum(keep.astype(BF16), lf_ref[...])
    g_tot = jnp.concatenate(
        [jnp.broadcast_to(g[i * c:i * c + 1, :] if reverse else g[(i + 1) * c - 1:(i + 1) * c, :],
                          (c, g.shape[1])) for i in range(n_sub)], axis=0)
    qg = (q_ref[...].astype(F32) * jnp.exp(g)).astype(BF16)
    kf = k_ref[...].astype(F32)
    kg = (kf * jnp.exp(-g)).astype(BF16)
    kd = (kf * jnp.exp(g_tot - g)).astype(BF16)
    v = v_ref[...]
    order = list(reversed(range(n_sub))) if reverse else list(range(n_sub))
    e_tot = [jnp.exp(g_tot[i * c:i * c + 1, :]) for i in range(n_sub)]
    outs = []
    for h in range(HGRN_HEADS):
        cs = slice(h * HGRN_DK, (h + 1) * HGRN_DK)
        s = lax.dot_general(qg[:, cs], kg[:, cs], _NT, preferred_element_type=F32)
        p = jnp.where(keep, s, 0.0).astype(BF16)
        o_h = jnp.dot(p, v[:, cs], preferred_element_type=F32)
        st = s_scr[h]
        pieces = [None] * n_sub
        for i in order:
            rows = slice(i * c, (i + 1) * c)
            pieces[i] = o_h[rows, :] + lax.dot_general(
                qg[rows, cs], st.astype(BF16), _NT, preferred_element_type=F32)
            st = st * e_tot[i][:, cs] + lax.dot_general(
                v[rows, cs], kd[rows, cs], _TN, preferred_element_type=F32)
        s_scr[h] = st
        outs.append(jnp.concatenate(pieces, axis=0))
    return jnp.concatenate(outs, axis=-1)


def _hgrn_bwd_kernel(q_ref, v_ref, k_ref, lf_ref, ob_ref, s_scr):
    @pl.when(pl.program_id(1) == 0)
    def _():
        s_scr[...] = jnp.zeros_like(s_scr)

    ob_ref[...] = _hgrn_block(q_ref, k_ref, v_ref, lf_ref, s_scr, reverse=True).astype(BF16)


def _hgrn_fwd_kernel(q_ref, v_ref, k_ref, lf_ref, sg_ref, ob_ref, x_ref, nw_ref, wo_ref,
                     xy_ref, s_scr, o_scr):
    @pl.when(pl.program_id(1) == 0)
    def _():
        s_scr[...] = jnp.zeros_like(s_scr)

    o_scr[...] = (_hgrn_block(q_ref, k_ref, v_ref, lf_ref, s_scr, reverse=False)
                  + ob_ref[...].astype(F32))
    on = _rms_rows(o_scr[...], nw_ref[...])
    gated = (on * sg_ref[...].astype(F32)).astype(BF16)
    xn = (_load_tok(x_ref, X_ROWS, HGRN_BLOCK)
          + jnp.dot(gated, wo_ref[...], preferred_element_type=F32))
    _store_tok(xy_ref, 0, xn)
    _store_tok(xy_ref, X_ROWS, xn)


def _hgrn(a3, l3, xy3, norm_w, w_out):
    b, t, _ = a3.shape
    tb = HGRN_BLOCK
    nb = t // tb
    d = D_MODEL
    state = pltpu.VMEM((HGRN_HEADS, HGRN_DK, HGRN_DK), F32)

    ob = pl.pallas_call(
        _hgrn_bwd_kernel,
        grid=(b, nb),
        in_specs=[
            pl.BlockSpec((None, tb, d), lambda bi, i: (bi, nb - 1 - i, 0)),
            pl.BlockSpec((None, tb, d), lambda bi, i: (bi, nb - 1 - i, 1)),
            pl.BlockSpec((None, tb, d), lambda bi, i: (bi, nb - 1 - i, 3)),
            pl.BlockSpec((None, tb, d), lambda bi, i: (bi, nb - 1 - i, 1)),
        ],
        out_specs=pl.BlockSpec((None, tb, d), lambda bi, i: (bi, nb - 1 - i, 0)),
        out_shape=jax.ShapeDtypeStruct((b, t, d), BF16),
        scratch_shapes=[state],
        compiler_params=_cparams(("arbitrary", "arbitrary")),
        name="hgrn_bwd",
    )(a3, a3, a3, l3)

    return pl.pallas_call(
        _hgrn_fwd_kernel,
        grid=(b, nb),
        in_specs=[
            pl.BlockSpec((None, tb, d), lambda bi, i: (bi, i, 0)),
            pl.BlockSpec((None, tb, d), lambda bi, i: (bi, i, 1)),
            pl.BlockSpec((None, tb, d), lambda bi, i: (bi, i, 2)),
            pl.BlockSpec((None, tb, d), lambda bi, i: (bi, i, 0)),
            pl.BlockSpec((None, tb, d), lambda bi, i: (bi, i, 4)),
            pl.BlockSpec((None, tb, d), lambda bi, i: (bi, i, 0)),
            pl.BlockSpec((None, tb * TOK_ROWS, LANE), lambda bi, i: (bi, i, 0)),
            pl.BlockSpec((1, d), lambda bi, i: (0, 0)),
            pl.BlockSpec((d, d), lambda bi, i: (0, 0)),
        ],
        out_specs=pl.BlockSpec((None, tb * TOK_ROWS, LANE), lambda bi, i: (bi, i, 0)),
        out_shape=jax.ShapeDtypeStruct((b, t * TOK_ROWS, LANE), F32),
        scratch_shapes=[state, pltpu.VMEM((tb, d), F32)],
        compiler_params=_cparams(("arbitrary", "arbitrary")),
        name="hgrn_fwd",
    )(a3, a3, a3, l3, a3, ob, xy3, norm_w, w_out)


def _router_kernel(xy_ref, nw_ref, wr_ref, aff_ref):
    xn = _rms_rows(_load_tok(xy_ref, 0, PROJ_TM), nw_ref[...])
    xh = xn.astype(BF16)
    xl = (xn - xh.astype(F32)).astype(BF16)
    wr = wr_ref[...]
    wh = wr.astype(BF16)
    wl = (wr - wh.astype(F32)).astype(BF16)
    logits = (lax.dot_general(wh, xh, _NT, preferred_element_type=F32)
              + lax.dot_general(wh, xl, _NT, preferred_element_type=F32)
              + lax.dot_general(wl, xh, _NT, preferred_element_type=F32))
    m = jnp.max(logits, axis=0, keepdims=True)
    e = jnp.exp(logits - m)
    aff_ref[...] = e / jnp.sum(e, axis=0, keepdims=True)


def _router(xy, nw, wr_t):
    n = xy.shape[0] // TOK_ROWS
    return pl.pallas_call(
        _router_kernel,
        grid=(n // PROJ_TM,),
        in_specs=[
            pl.BlockSpec((PROJ_TM * TOK_ROWS, LANE), lambda i: (i, 0)),
            pl.BlockSpec((1, D_MODEL), lambda i: (0, 0)),
            pl.BlockSpec((N_EXPERTS, D_MODEL), lambda i: (0, 0)),
        ],
        out_specs=pl.BlockSpec((N_EXPERTS, PROJ_TM), lambda i: (0, i)),
        out_shape=jax.ShapeDtypeStruct((N_EXPERTS, n), F32),
        compiler_params=_cparams(("arbitrary",)),
        name="router",
    )(xy, nw, wr_t)


def _ffn_kernel(idx_ref, idxf_ref, idxp_ref, idxn_ref, gate_ref, nw_ref, wg_ref, wu_ref, wd_ref,
                xy_in_ref, xy_ref, buf, work, xn_scr, sem_g, sem_s, *, tiles_per_expert):
    del xy_in_ref
    tm = FFN_TM
    pitch = FFN_PITCH
    e = pl.program_id(0)
    i = pl.program_id(1)
    step = e * tiles_per_expert + i
    n_steps = pl.num_programs(0) * tiles_per_expert
    last_tile = i == tiles_per_expert - 1
    acc0 = 2 * tm * X_ROWS

    def start_gather(ids, dst_slot, lo=0, hi=FFN_TM):
        dst = dst_slot * (tm * pitch)
        for r in range(lo, hi):
            t = pl.multiple_of(ids[0, r] * TOK_ROWS, TOK_ROWS)
            pltpu.make_async_copy(xy_ref.at[pl.ds(t, TOK_ROWS), :],
                                  buf.at[pl.ds(dst + r * pitch, TOK_ROWS), :],
                                  sem_g.at[dst_slot]).start()

    def wait_gather(dst_slot):
        pltpu.make_async_copy(xy_ref.at[pl.ds(0, tm * TOK_ROWS), :],
                              buf.at[pl.ds(0, tm * TOK_ROWS), :], sem_g.at[dst_slot]).wait()

    def start_scatter(ids, src, lo=0, hi=FFN_TM):
        for r in range(lo, hi):
            t = pl.multiple_of(ids[0, r] * TOK_ROWS + X_ROWS, X_ROWS)
            pltpu.make_async_copy(work.at[pl.ds(src + r * X_ROWS, X_ROWS), :],
                                  xy_ref.at[pl.ds(t, X_ROWS), :], sem_s.at[0]).start()

    def wait_scatter():
        pltpu.make_async_copy(work.at[pl.ds(0, tm * X_ROWS), :],
                              xy_ref.at[pl.ds(0, tm * X_ROWS), :], sem_s.at[0]).wait()

    @pl.when(step == 0)
    def _():
        start_gather(idx_ref, 0)

    def run(slot):
        base = slot * (tm * pitch)
        st_cur = slot * (tm * X_ROWS)
        st_prev = (1 - slot) * (tm * X_ROWS)
        wait_gather(slot)

        @pl.when(i == 0)
        def _():
            for j in range(X_ROWS):
                work[pl.ds(st_prev + j, tm, stride=X_ROWS), :] = (
                    buf[pl.ds(base + X_ROWS + j, tm, stride=pitch), :])

        xn_scr[...] = _rms_rows(_load_tok(buf, base, tm, pitch), nw_ref[...]).astype(BF16)

        n_f = EXPERT_FF // FFN_FW
        n_pts = 3 * n_f
        cuts = [(tm * k) // n_pts for k in range(n_pts + 1)]

        def issue(k):
            start_scatter(idxp_ref, st_prev, cuts[k], cuts[k + 1])
            start_gather(idxf_ref, 1 - slot, cuts[k], cuts[k + 1])

        eye = (lax.broadcasted_iota(jnp.int32, (LANE, LANE), 0)
               == lax.broadcasted_iota(jnp.int32, (LANE, LANE), 1)).astype(F32)
        gate_col = jnp.concatenate(
            [jnp.sum(eye * gate_ref[r:r + 1, :], axis=1, keepdims=True)
             for r in range(tm // LANE)], axis=0)
        for f in range(n_f):
            fs = slice(f * FFN_FW, (f + 1) * FFN_FW)
            issue(3 * f)
            g = jnp.dot(xn_scr[...], wg_ref[:, fs], preferred_element_type=F32)
            issue(3 * f + 1)
            u = jnp.dot(xn_scr[...], wu_ref[:, fs], preferred_element_type=F32)
            hcur = (g * jax.nn.sigmoid(g) * u * gate_col).astype(BF16)
            issue(3 * f + 2)
            part = jnp.dot(hcur, wd_ref[fs, :], preferred_element_type=F32)
            if f < n_f - 1:
                for j in range(X_ROWS):
                    rows = pl.ds(acc0 + j * tm, tm)
                    cs = slice(j * LANE, (j + 1) * LANE)
                    if f == 0:
                        work[rows, :] = part[:, cs]
                    elif f == 1:
                        work[rows, :] = (work[rows, :] + part[:, cs]
                                         + buf[pl.ds(base + X_ROWS + j, tm, stride=pitch), :])
                    else:
                        work[rows, :] += part[:, cs]
            else:
                wait_scatter()
                for j in range(X_ROWS):
                    cs = slice(j * LANE, (j + 1) * LANE)
                    work[pl.ds(st_cur + j, tm, stride=X_ROWS), :] = (
                        work[pl.ds(acc0 + j * tm, tm), :] + part[:, cs])

        @pl.when(last_tile)
        def _():
            wait_gather(1 - slot)
            start_scatter(idx_ref, st_cur)
            wait_scatter()

            @pl.when(step < n_steps - 1)
            def _():
                start_gather(idxn_ref, 1 - slot)

    for slot in (0, 1):
        pl.when(step % 2 == slot)(functools.partial(run, slot))


def _expert_ffn(xy, idx, gate, nw, wg, wu, wd):
    cap = idx.shape[1]
    tm = FFN_TM
    nt = cap // tm
    assert nt >= 2 and EXPERT_FF // FFN_FW >= 3
    n_steps = N_EXPERTS * nt
    idx3 = idx.reshape(n_steps, 1, tm)
    gate3 = gate.reshape(n_steps, tm // LANE, LANE)
    smem_tile = functools.partial(pl.BlockSpec, (None, 1, tm), memory_space=pltpu.SMEM)
    return pl.pallas_call(
        functools.partial(_ffn_kernel, tiles_per_expert=nt),
        grid=(N_EXPERTS, nt),
        in_specs=[
            smem_tile(lambda e, i: (e * nt + i, 0, 0)),
            smem_tile(lambda e, i: (e * nt + jnp.minimum(i + 1, nt - 1), 0, 0)),
            smem_tile(lambda e, i: (e * nt + jnp.maximum(i - 1, 0), 0, 0)),
            smem_tile(lambda e, i: (jnp.minimum(e * nt + i + 1, n_steps - 1), 0, 0)),
            pl.BlockSpec((None, tm // LANE, LANE), lambda e, i: (e * nt + i, 0, 0)),
            pl.BlockSpec((1, D_MODEL), lambda e, i: (0, 0)),
            pl.BlockSpec((None, D_MODEL, EXPERT_FF), lambda e, i: (e, 0, 0)),
            pl.BlockSpec((None, D_MODEL, EXPERT_FF), lambda e, i: (e, 0, 0)),
            pl.BlockSpec((None, EXPERT_FF, D_MODEL), lambda e, i: (e, 0, 0)),
            pl.BlockSpec(memory_space=pl.ANY),
        ],
        out_specs=pl.BlockSpec(memory_space=pl.ANY),
        out_shape=jax.ShapeDtypeStruct(xy.shape, F32),
        scratch_shapes=[
            pltpu.VMEM((2 * tm * FFN_PITCH, LANE), F32),
            pltpu.VMEM((3 * tm * X_ROWS, LANE), F32),
            pltpu.VMEM((tm, D_MODEL), BF16),
            pltpu.SemaphoreType.DMA((2,)),
            pltpu.SemaphoreType.DMA((1,)),
        ],
        input_output_aliases={9: 0},
        compiler_params=_cparams(("arbitrary", "arbitrary")),
        name="expert_ffn",
    )(idx3, idx3, idx3, idx3, gate3, nw, wg, wu, wd, xy)


def _threshold_kernel(aff_ref, thr_ref, need_ref, *, k):
    bits = lax.bitcast_convert_type(aff_ref[...], I32)

    def body(b, t):
        cand = t | (jnp.int32(1) << (30 - b))
        cnt = jnp.sum((bits >= cand).astype(I32), axis=1, keepdims=True)
        return jnp.where(cnt >= k, cand, t)

    t = lax.fori_loop(0, 31, body, jnp.zeros((N_EXPERTS, 1), I32))
    n_gt = jnp.sum((bits > t).astype(I32), axis=1, keepdims=True)
    thr_ref[...] = jnp.broadcast_to(lax.bitcast_convert_type(t, F32), thr_ref.shape)
    need_ref[...] = jnp.broadcast_to(k - n_gt, need_ref.shape)


def _threshold(aff, k):
    return pl.pallas_call(
        functools.partial(_threshold_kernel, k=k),
        out_shape=[jax.ShapeDtypeStruct((N_EXPERTS, LANE), F32),
                   jax.ShapeDtypeStruct((N_EXPERTS, LANE), I32)],
        compiler_params=pltpu.CompilerParams(vmem_limit_bytes=VMEM_LIMIT_BYTES),
        name="topk_threshold",
    )(aff)


def _compact_sc(aff_flat, thr_flat, need_flat, n, cap):
    lanes = SC_LANES
    mesh = plsc.VectorSubcoreMesh(core_axis_name="c", subcore_axis_name="s", num_cores=1,
                                  num_subcores=N_EXPERTS)

    @functools.partial(
        pl.kernel,
        out_type=(jax.ShapeDtypeStruct((N_EXPERTS * cap,), I32),
                  jax.ShapeDtypeStruct((N_EXPERTS * cap,), F32)),
        mesh=mesh,
        scratch_types=[pltpu.VMEM((n,), F32), pltpu.VMEM((cap,), I32), pltpu.VMEM((cap,), F32),
                       pltpu.VMEM((lanes,), F32), pltpu.VMEM((lanes,), I32)],
        compiler_params=pltpu.CompilerParams(needs_layout_passes=False),
        name="topk_compact",
    )
    def compact(aff_hbm, thr_hbm, need_hbm, idx_hbm, gate_hbm, row, idx_v, gate_v, thr_v, need_v):
        e = lax.axis_index("s")
        pltpu.sync_copy(aff_hbm.at[pl.ds(e * n, n)], row)
        pltpu.sync_copy(thr_hbm.at[pl.ds(e * LANE, lanes)], thr_v)
        pltpu.sync_copy(need_hbm.at[pl.ds(e * LANE, lanes)], need_v)
        thr = thr_v[...]
        need = need_v[...]
        lane = lax.iota(I32, lanes)
        zero_i = jnp.zeros((lanes,), I32)

        def init(i, carry):
            idx_v[pl.ds(i * lanes, lanes)] = zero_i
            gate_v[pl.ds(i * lanes, lanes)] = jnp.zeros((lanes,), F32)
            return carry

        lax.fori_loop(0, cap // lanes, init, 0)

        def body(i, carry):
            off, eqs = carry
            v = row[pl.ds(i * lanes, lanes)]
            m_gt = v > thr
            m_eq = v == thr
            eq_rank = plsc.cumsum(m_eq.astype(I32)) + eqs
            take = m_gt | (m_eq & (eq_rank <= need))
            pos = off + plsc.cumsum(take.astype(I32)) - 1
            take = take & (pos < cap)
            plsc.store_scatter(idx_v, [pos], lane + i * lanes, mask=take)
            plsc.store_scatter(gate_v, [pos], v, mask=take)
            off = off + plsc.all_reduce_population_count(take)
            eqs = eqs + plsc.all_reduce_population_count(m_eq)
            return off, eqs

        lax.fori_loop(0, n // lanes, body, (zero_i, zero_i))
        pltpu.sync_copy(idx_v, idx_hbm.at[pl.ds(e * cap, cap)])
        pltpu.sync_copy(gate_v, gate_hbm.at[pl.ds(e * cap, cap)])

    return compact(aff_flat, thr_flat, need_flat)


def _expert_choice(aff, cap):
    n = aff.shape[1]
    thr, need = _threshold(aff, cap)
    idx, gate = _compact_sc(aff.reshape(-1), thr.reshape(-1), need.reshape(-1), n, cap)
    return gate.reshape(N_EXPERTS, cap), idx.reshape(N_EXPERTS, cap)


def _moe(xy, nw, wr_t, wg, wu, wd):
    n = xy.shape[0] // TOK_ROWS
    cap = EC_CAPACITY_FACTOR * n // N_EXPERTS
    aff = _router(xy, nw, wr_t)
    gate, idx = _expert_choice(aff, cap)
    return _expert_ffn(xy, idx, gate, nw, wg, wu, wd)


def _final_norm_kernel(xy_ref, nw_ref, o_ref):
    o_ref[...] = _rms_rows(_load_tok(xy_ref, X_ROWS, PROJ_TM), nw_ref[...])


def _final_norm(xy, nw):
    n = xy.shape[0] // TOK_ROWS
    return pl.pallas_call(
        _final_norm_kernel,
        grid=(n // PROJ_TM,),
        in_specs=[pl.BlockSpec((PROJ_TM * TOK_ROWS, LANE), lambda i: (i, 0)),
                  pl.BlockSpec((1, D_MODEL), lambda i: (0, 0))],
        out_specs=pl.BlockSpec((PROJ_TM, D_MODEL), lambda i: (i, 0)),
        out_shape=jax.ShapeDtypeStruct((n, D_MODEL), F32),
        compiler_params=_cparams(("arbitrary",)),
        name="final_norm",
    )(xy, nw)


def _rope_tables(seq):
    half = RET_DK // 2
    inv = 1.0 / (ROPE_BASE ** (jnp.arange(half, dtype=F32) / half))
    ang = jnp.arange(seq, dtype=F32)[:, None] * inv[None, :]
    return jnp.cos(ang), jnp.sin(ang)


def _trunk(x, p):
    b, t, d = x.shape
    n = b * t
    cos, sin = _rope_tables(t)

    proj = _ret_inproj(x.reshape(n, d), 0, p["norm_mix"][0], p["ret_w_in"], cos, sin, t)
    xy = _retention(proj.reshape(b, t, -1), x, p["ret_decay"], p["ret_w_out"])
    xy = _moe(xy.reshape(n * TOK_ROWS, LANE), p["norm_ffn"][0], p["w_router_t"][0],
              p["w_gate"][0], p["w_up"][0], p["w_down"][0])

    a, lf = _hgrn_inproj(xy, p["norm_mix"][1], p["hgrn_w_in"], p["hgrn_lb"])
    xy = _hgrn(a.reshape(b, t, -1), lf.reshape(b, t, -1), xy.reshape(b, t * TOK_ROWS, LANE),
               p["hgrn_norm"], p["hgrn_w_out"])
    xy = _moe(xy.reshape(n * TOK_ROWS, LANE), p["norm_ffn"][1], p["w_router_t"][1],
              p["w_gate"][1], p["w_up"][1], p["w_down"][1])

    return _final_norm(xy, p["norm_final"]).reshape(b, t, d)


def kernel(x_prompt, x_sample, norm_mix_w, norm_ffn_w, norm_final_w, ret_w_in, ret_w_out,
           hgrn_w_in, hgrn_lb, hgrn_norm_w, hgrn_w_out, moe_w_router, moe_w_gate, moe_w_up,
           moe_w_down):
    depth = norm_mix_w.shape[0]
    assert depth == 2 and ret_w_in.shape[0] == 1 and hgrn_w_in.shape[0] == 1
    sm = jax.nn.softmax(hgrn_lb.astype(F32), axis=0)
    lower_bounds = jnp.cumsum(sm, axis=0) - sm[0:1]
    p = {
        "norm_mix": norm_mix_w.reshape(depth, 1, D_MODEL),
        "norm_ffn": norm_ffn_w.reshape(depth, 1, D_MODEL),
        "norm_final": norm_final_w.reshape(1, D_MODEL),
        "ret_w_in": ret_w_in[0].astype(BF16),
        "ret_w_out": ret_w_out[0].astype(BF16),
        "ret_decay": _ret_decay_matrix(),
        "hgrn_w_in": hgrn_w_in[0].astype(BF16),
        "hgrn_lb": lower_bounds[1].reshape(1, 2 * D_MODEL),
        "hgrn_norm": hgrn_norm_w[0].reshape(1, D_MODEL),
        "hgrn_w_out": hgrn_w_out[0].astype(BF16),
        "w_router_t": jnp.swapaxes(moe_w_router, 1, 2),
        "w_gate": [moe_w_gate[l].astype(BF16) for l in range(depth)],
        "w_up": [moe_w_up[l].astype(BF16) for l in range(depth)],
        "w_down": [moe_w_down[l].astype(BF16) for l in range(depth)],
    }
    return _trunk(x_prompt, p), _trunk(x_sample, p)
```

```python
import functools
import math

import jax
import jax.numpy as jnp
from jax import lax
from jax.experimental import pallas as pl
from jax.experimental.pallas import tpu as pltpu
from jax.experimental.pallas import tpu_sc as plsc

F32 = jnp.float32
I32 = jnp.int32
BF16 = jnp.bfloat16

D_MODEL = 1024
NORM_EPS = 1e-6
ROPE_BASE = 10000.0

RET_HEADS = 4
RET_DK = 256
RET_DV = 512
RET_QK_WIDTH = RET_HEADS * RET_DK
RET_V_WIDTH = RET_HEADS * RET_DV
RET_DECAY_FWD = 5.0
RET_DECAY_BWD = 5.5
RET_BLOCK = 256

HGRN_HEADS = 8
HGRN_DK = 128
HGRN_CHUNK = 64
HGRN_BLOCK = 256

N_EXPERTS = 16
EC_CAPACITY_FACTOR = 2
EXPERT_FF = 2 * D_MODEL

PROJ_TM = 512
PROJ_CW = 512
FFN_TM = 512
FFN_FW = 512
LANE = 128
SC_LANES = 16
X_ROWS = D_MODEL // LANE
TOK_ROWS = 2 * X_ROWS

VMEM_LIMIT_BYTES = 52 * 1024 * 1024


def _cparams(sem):
    return pltpu.CompilerParams(dimension_semantics=sem, vmem_limit_bytes=VMEM_LIMIT_BYTES)


def _ret_log_gamma(offset):
    return [math.log1p(-(2.0 ** (-offset - h))) for h in range(RET_HEADS)]


def _rms_rows(x, w):
    ms = jnp.mean(x * x, axis=-1, keepdims=True)
    return x * lax.rsqrt(ms + NORM_EPS) * w


def _from_tok_rows(v):
    return v.reshape(v.shape[0], D_MODEL)


def _to_tok_rows(v):
    return v.reshape(v.shape[0], X_ROWS, LANE)


def _store_xy(xy_ref, val):
    r = _to_tok_rows(val)
    xy_ref[:, 0:X_ROWS, :] = r
    xy_ref[:, X_ROWS:TOK_ROWS, :] = r


def _ret_inproj_kernel(x_ref, nw_ref, w_ref, cos_ref, sin_ref, o_ref, xn_scr):
    xn_scr[...] = _rms_rows(x_ref[...], nw_ref[...]).astype(BF16)
    cos = cos_ref[...]
    sin = sin_ref[...]
    width = w_ref.shape[1]
    half = RET_DK // 2
    for j in range(width // PROJ_CW):
        c0 = j * PROJ_CW
        r = jnp.dot(xn_scr[...], w_ref[:, c0:c0 + PROJ_CW], preferred_element_type=F32)
        if c0 < 2 * RET_QK_WIDTH:
            scale = 1.0 if c0 < RET_QK_WIDTH else RET_DK ** -0.5
            parts = []
            for hh in range(PROJ_CW // RET_DK):
                x1 = r[:, hh * RET_DK:hh * RET_DK + half]
                x2 = r[:, hh * RET_DK + half:(hh + 1) * RET_DK]
                parts.append((x1 * cos - x2 * sin) * scale)
                parts.append((x1 * sin + x2 * cos) * scale)
            r = jnp.concatenate(parts, axis=-1)
        elif c0 >= 2 * RET_QK_WIDTH + RET_V_WIDTH:
            r = r * jax.nn.sigmoid(r)
        o_ref[:, c0:c0 + PROJ_CW] = r.astype(BF16)


def _ret_inproj(x2d, col, nw, w, cos, sin, seq):
    n = x2d.shape[0]
    width = w.shape[1]
    tpb = seq // PROJ_TM
    return pl.pallas_call(
        _ret_inproj_kernel,
        grid=(n // PROJ_TM,),
        in_specs=[
            pl.BlockSpec((PROJ_TM, D_MODEL), lambda i: (i, col)),
            pl.BlockSpec((1, D_MODEL), lambda i: (0, 0)),
            pl.BlockSpec((D_MODEL, width), lambda i: (0, 0)),
            pl.BlockSpec((PROJ_TM, RET_DK // 2), lambda i: (i % tpb, 0)),
            pl.BlockSpec((PROJ_TM, RET_DK // 2), lambda i: (i % tpb, 0)),
        ],
        out_specs=pl.BlockSpec((PROJ_TM, width), lambda i: (i, 0)),
        out_shape=jax.ShapeDtypeStruct((n, width), BF16),
        scratch_shapes=[pltpu.VMEM((PROJ_TM, D_MODEL), BF16)],
        compiler_params=_cparams(("arbitrary",)),
        name="ret_inproj",
    )(x2d, nw, w, cos, sin)


def _hgrn_inproj_kernel(xy_ref, nw_ref, w_ref, lb_ref, a_ref, l_ref, xn_scr):
    xn_scr[...] = _rms_rows(_from_tok_rows(xy_ref[...]), nw_ref[...]).astype(BF16)
    width = w_ref.shape[1]
    for j in range(width // PROJ_CW):
        c0 = j * PROJ_CW
        r = jnp.dot(xn_scr[...], w_ref[:, c0:c0 + PROJ_CW], preferred_element_type=F32)
        if c0 < D_MODEL:
            a_ref[:, c0:c0 + PROJ_CW] = (r * jax.nn.sigmoid(r)).astype(BF16)
        elif c0 < 2 * D_MODEL:
            a_ref[:, c0:c0 + PROJ_CW] = r.astype(BF16)
        elif c0 < 4 * D_MODEL:
            g0 = c0 - 2 * D_MODEL
            lb = lb_ref[:, g0:g0 + PROJ_CW]
            f = lb + (1.0 - lb) * jax.nn.sigmoid(r)
            a_ref[:, c0:c0 + PROJ_CW] = (1.0 - f).astype(BF16)
            l_ref[:, g0:g0 + PROJ_CW] = jnp.log(f)
        else:
            a_ref[:, c0:c0 + PROJ_CW] = jax.nn.sigmoid(r).astype(BF16)


def _hgrn_inproj(xy, nw, w, lb):
    n = xy.shape[0]
    width = w.shape[1]
    return pl.pallas_call(
        _hgrn_inproj_kernel,
        grid=(n // PROJ_TM,),
        in_specs=[
            pl.BlockSpec((PROJ_TM, X_ROWS, LANE), lambda i: (i, 1, 0)),
            pl.BlockSpec((1, D_MODEL), lambda i: (0, 0)),
            pl.BlockSpec((D_MODEL, width), lambda i: (0, 0)),
            pl.BlockSpec((1, 2 * D_MODEL), lambda i: (0, 0)),
        ],
        out_specs=[
            pl.BlockSpec((PROJ_TM, width), lambda i: (i, 0)),
            pl.BlockSpec((PROJ_TM, 2 * D_MODEL), lambda i: (i, 0)),
        ],
        out_shape=[
            jax.ShapeDtypeStruct((n, width), BF16),
            jax.ShapeDtypeStruct((n, 2 * D_MODEL), F32),
        ],
        scratch_shapes=[pltpu.VMEM((PROJ_TM, D_MODEL), BF16)],
        compiler_params=_cparams(("arbitrary",)),
        name="hgrn_inproj",
    )(xy, nw, w, lb)


_NT = (((1,), (1,)), ((), ()))
_TN = (((0,), (0,)), ((), ()))


def _ret_bwd_kernel(q_ref, k_ref, v_ref, ob_ref, r_scr, *, lg):
    @pl.when(pl.program_id(1) == 0)
    def _():
        r_scr[...] = jnp.zeros_like(r_scr)

    c = q_ref.shape[0]
    pos = lax.broadcasted_iota(jnp.int32, (c, 1), 0).astype(F32)
    for h in range(RET_HEADS):
        qh = q_ref[:, h * RET_DK:(h + 1) * RET_DK].astype(F32)
        kh = k_ref[:, h * RET_DK:(h + 1) * RET_DK].astype(F32)
        vh = v_ref[:, h * RET_DV:(h + 1) * RET_DV]
        qd = (qh * jnp.exp((c - pos) * lg[h])).astype(BF16)
        o = jnp.dot(qd, r_scr[h].astype(BF16), preferred_element_type=F32)
        ob_ref[:, h * RET_DV:(h + 1) * RET_DV] = o.astype(BF16)
        kd = (kh * jnp.exp(pos * lg[h])).astype(BF16)
        r_scr[h] = r_scr[h] * math.exp(c * lg[h]) + lax.dot_general(
            kd, vh, _TN, preferred_element_type=F32)


def _ret_fwd_kernel(q_ref, k_ref, v_ref, g_ref, ob_ref, x_ref, m_ref, wo_ref, xy_ref,
                    r_scr, *, lg):
    @pl.when(pl.program_id(1) == 0)
    def _():
        r_scr[...] = jnp.zeros_like(r_scr)

    c = q_ref.shape[0]
    pos = lax.broadcasted_iota(jnp.int32, (c, 1), 0).astype(F32)
    acc = x_ref[...]
    for h in range(RET_HEADS):
        qb = q_ref[:, h * RET_DK:(h + 1) * RET_DK]
        kb = k_ref[:, h * RET_DK:(h + 1) * RET_DK]
        vh = v_ref[:, h * RET_DV:(h + 1) * RET_DV]
        s = lax.dot_general(qb, kb, _NT, preferred_element_type=F32)
        sm = (s * m_ref[h]).astype(BF16)
        qd = (qb.astype(F32) * jnp.exp((pos + 1.0) * lg[h])).astype(BF16)
        o = (jnp.dot(sm, vh, preferred_element_type=F32)
             + jnp.dot(qd, r_scr[h].astype(BF16), preferred_element_type=F32)
             + ob_ref[:, h * RET_DV:(h + 1) * RET_DV].astype(F32))
        kd = (kb.astype(F32) * jnp.exp((c - 1.0 - pos) * lg[h])).astype(BF16)
        r_scr[h] = r_scr[h] * math.exp(c * lg[h]) + lax.dot_general(
            kd, vh, _TN, preferred_element_type=F32)
        ms = jnp.mean(o * o, axis=-1, keepdims=True)
        gated = (o * lax.rsqrt(ms + NORM_EPS)
                 * g_ref[:, h * RET_DV:(h + 1) * RET_DV].astype(F32)).astype(BF16)
        acc = acc + jnp.dot(gated, wo_ref[h * RET_DV:(h + 1) * RET_DV, :],
                            preferred_element_type=F32)
    _store_xy(xy_ref, acc)


def _retention(proj, x3, decay_mat, w_out):
    b, t, _ = proj.shape
    c = RET_BLOCK
    nc = t // c
    lgf = _ret_log_gamma(RET_DECAY_FWD)
    lgb = _ret_log_gamma(RET_DECAY_BWD)
    qk = RET_QK_WIDTH
    vw = RET_V_WIDTH
    state = pltpu.VMEM((RET_HEADS, RET_DK, RET_DV), F32)

    ob = pl.pallas_call(
        functools.partial(_ret_bwd_kernel, lg=lgb),
        grid=(b, nc),
        in_specs=[
            pl.BlockSpec((None, c, qk), lambda bi, i: (bi, nc - 1 - i, 0)),
            pl.BlockSpec((None, c, qk), lambda bi, i: (bi, nc - 1 - i, 1)),
            pl.BlockSpec((None, c, vw), lambda bi, i: (bi, nc - 1 - i, 1)),
        ],
        out_specs=pl.BlockSpec((None, c, vw), lambda bi, i: (bi, nc - 1 - i, 0)),
        out_shape=jax.ShapeDtypeStruct((b, t, vw), BF16),
        scratch_shapes=[state],
        compiler_params=_cparams(("arbitrary", "arbitrary")),
        name="ret_bwd",
    )(proj, proj, proj)

    return pl.pallas_call(
        functools.partial(_ret_fwd_kernel, lg=lgf),
        grid=(b, nc),
        in_specs=[
            pl.BlockSpec((None, c, qk), lambda bi, i: (bi, i, 0)),
            pl.BlockSpec((None, c, qk), lambda bi, i: (bi, i, 1)),
            pl.BlockSpec((None, c, vw), lambda bi, i: (bi, i, 1)),
            pl.BlockSpec((None, c, vw), lambda bi, i: (bi, i, 2)),
            pl.BlockSpec((None, c, vw), lambda bi, i: (bi, i, 0)),
            pl.BlockSpec((None, c, D_MODEL), lambda bi, i: (bi, i, 0)),
            pl.BlockSpec((RET_HEADS, c, c), lambda bi, i: (0, 0, 0)),
            pl.BlockSpec((vw, D_MODEL), lambda bi, i: (0, 0)),
        ],
        out_specs=pl.BlockSpec((None, c, TOK_ROWS, LANE), lambda bi, i: (bi, i, 0, 0)),
        out_shape=jax.ShapeDtypeStruct((b, t, TOK_ROWS, LANE), F32),
        scratch_shapes=[state],
        compiler_params=_cparams(("arbitrary", "arbitrary")),
        name="ret_fwd",
    )(proj, proj, proj, proj, ob, x3, decay_mat, w_out)


def _ret_decay_matrix():
    c = RET_BLOCK
    rel = (jnp.arange(c, dtype=F32)[:, None] - jnp.arange(c, dtype=F32)[None, :])[None]
    lgf = jnp.asarray(_ret_log_gamma(RET_DECAY_FWD), F32)[:, None, None]
    lgb = jnp.asarray(_ret_log_gamma(RET_DECAY_BWD), F32)[:, None, None]
    return jnp.where(rel >= 0, jnp.exp(jnp.maximum(rel, 0.0) * lgf),
                     jnp.exp(jnp.maximum(-rel, 0.0) * lgb))


def _tri_cumsum(tri, x):
    hi = x.astype(BF16)
    lo = (x - hi.astype(F32)).astype(BF16)
    return (jnp.dot(tri, hi, preferred_element_type=F32)
            + jnp.dot(tri, lo, preferred_element_type=F32))


def _hgrn_block(q_ref, k_ref, v_ref, lf_ref, s_scr, reverse):
    tb = HGRN_BLOCK
    c = HGRN_CHUNK
    n_sub = tb // c
    shift = c.bit_length() - 1
    row = lax.broadcasted_iota(jnp.int32, (tb, tb), 0)
    col = lax.broadcasted_iota(jnp.int32, (tb, tb), 1)
    same_chunk = lax.shift_right_logical(row, shift) == lax.shift_right_logical(col, shift)
    keep = same_chunk & ((col >= row) if reverse else (col <= row))
    g = _tri_cumsum(keep.astype(BF16), lf_ref[...])
    g_tot = jnp.concatenate(
        [jnp.broadcast_to(g[i * c:i * c + 1, :] if reverse else g[(i + 1) * c - 1:(i + 1) * c, :],
                          (c, g.shape[1])) for i in range(n_sub)], axis=0)
    qg = (q_ref[...].astype(F32) * jnp.exp(g)).astype(BF16)
    kf = k_ref[...].astype(F32)
    kg = (kf * jnp.exp(-g)).astype(BF16)
    kd = (kf * jnp.exp(g_tot - g)).astype(BF16)
    v = v_ref[...]
    order = list(reversed(range(n_sub))) if reverse else list(range(n_sub))
    e_tot = [jnp.exp(g_tot[i * c:i * c + 1, :]) for i in range(n_sub)]
    outs = []
    for h in range(HGRN_HEADS):
        cs = slice(h * HGRN_DK, (h + 1) * HGRN_DK)
        s = lax.dot_general(qg[:, cs], kg[:, cs], _NT, preferred_element_type=F32)
        p = jnp.where(keep, s, 0.0).astype(BF16)
        o_h = jnp.dot(p, v[:, cs], preferred_element_type=F32)
        st = s_scr[h]
        pieces = [None] * n_sub
        for i in order:
            rows = slice(i * c, (i + 1) * c)
            pieces[i] = o_h[rows, :] + lax.dot_general(
                qg[rows, cs], st.astype(BF16), _NT, preferred_element_type=F32)
            st = st * e_tot[i][:, cs] + lax.dot_general(
                v[rows, cs], kd[rows, cs], _TN, preferred_element_type=F32)
        s_scr[h] = st
        outs.append(jnp.concatenate(pieces, axis=0))
    return jnp.concatenate(outs, axis=-1)


def _hgrn_bwd_kernel(q_ref, v_ref, k_ref, lf_ref, ob_ref, s_scr):
    @pl.when(pl.program_id(1) == 0)
    def _():
        s_scr[...] = jnp.zeros_like(s_scr)

    ob_ref[...] = _hgrn_block(q_ref, k_ref, v_ref, lf_ref, s_scr, reverse=True).astype(BF16)


def _hgrn_fwd_kernel(q_ref, v_ref, k_ref, lf_ref, sg_ref, ob_ref, x_ref, nw_ref, wo_ref,
                     xy_ref, s_scr, o_scr):
    @pl.when(pl.program_id(1) == 0)
    def _():
        s_scr[...] = jnp.zeros_like(s_scr)

    o_scr[...] = (_hgrn_block(q_ref, k_ref, v_ref, lf_ref, s_scr, reverse=False)
                  + ob_ref[...].astype(F32))
    on = _rms_rows(o_scr[...], nw_ref[...])
    gated = (on * sg_ref[...].astype(F32)).astype(BF16)
    xn = (_from_tok_rows(x_ref[...])
          + jnp.dot(gated, wo_ref[...], preferred_element_type=F32))
    _store_xy(xy_ref, xn)


def _hgrn(a3, l3, xy3, norm_w, w_out):
    b, t, _ = a3.shape
    tb = HGRN_BLOCK
    nb = t // tb
    d = D_MODEL
    state = pltpu.VMEM((HGRN_HEADS, HGRN_DK, HGRN_DK), F32)

    ob = pl.pallas_call(
        _hgrn_bwd_kernel,
        grid=(b, nb),
        in_specs=[
            pl.BlockSpec((None, tb, d), lambda bi, i: (bi, nb - 1 - i, 0)),
            pl.BlockSpec((None, tb, d), lambda bi, i: (bi, nb - 1 - i, 1)),
            pl.BlockSpec((None, tb, d), lambda bi, i: (bi, nb - 1 - i, 3)),
            pl.BlockSpec((None, tb, d), lambda bi, i: (bi, nb - 1 - i, 1)),
        ],
        out_specs=pl.BlockSpec((None, tb, d), lambda bi, i: (bi, nb - 1 - i, 0)),
        out_shape=jax.ShapeDtypeStruct((b, t, d), BF16),
        scratch_shapes=[state],
        compiler_params=_cparams(("arbitrary", "arbitrary")),
        name="hgrn_bwd",
    )(a3, a3, a3, l3)

    return pl.pallas_call(
        _hgrn_fwd_kernel,
        grid=(b, nb),
        in_specs=[
            pl.BlockSpec((None, tb, d), lambda bi, i: (bi, i, 0)),
            pl.BlockSpec((None, tb, d), lambda bi, i: (bi, i, 1)),
            pl.BlockSpec((None, tb, d), lambda bi, i: (bi, i, 2)),
            pl.BlockSpec((None, tb, d), lambda bi, i: (bi, i, 0)),
            pl.BlockSpec((None, tb, d), lambda bi, i: (bi, i, 4)),
            pl.BlockSpec((None, tb, d), lambda bi, i: (bi, i, 0)),
            pl.BlockSpec((None, tb, X_ROWS, LANE), lambda bi, i: (bi, i, 1, 0)),
            pl.BlockSpec((1, d), lambda bi, i: (0, 0)),
            pl.BlockSpec((d, d), lambda bi, i: (0, 0)),
        ],
        out_specs=pl.BlockSpec((None, tb, TOK_ROWS, LANE), lambda bi, i: (bi, i, 0, 0)),
        out_shape=jax.ShapeDtypeStruct((b, t, TOK_ROWS, LANE), F32),
        scratch_shapes=[state, pltpu.VMEM((tb, d), F32)],
        compiler_params=_cparams(("arbitrary", "arbitrary")),
        name="hgrn_fwd",
    )(a3, a3, a3, l3, a3, ob, xy3, norm_w, w_out)


def _router_kernel(xy_ref, nw_ref, wr_ref, aff_ref):
    xn = _rms_rows(_from_tok_rows(xy_ref[...]), nw_ref[...])
    xh = xn.astype(BF16)
    xl = (xn - xh.astype(F32)).astype(BF16)
    wr = wr_ref[...]
    wh = wr.astype(BF16)
    wl = (wr - wh.astype(F32)).astype(BF16)
    logits = (lax.dot_general(wh, xh, _NT, preferred_element_type=F32)
              + lax.dot_general(wh, xl, _NT, preferred_element_type=F32)
              + lax.dot_general(wl, xh, _NT, preferred_element_type=F32))
    m = jnp.max(logits, axis=0, keepdims=True)
    e = jnp.exp(logits - m)
    aff_ref[...] = e / jnp.sum(e, axis=0, keepdims=True)


def _router(xy, nw, wr_t):
    n = xy.shape[0]
    return pl.pallas_call(
        _router_kernel,
        grid=(n // PROJ_TM,),
        in_specs=[
            pl.BlockSpec((PROJ_TM, X_ROWS, LANE), lambda i: (i, 0, 0)),
            pl.BlockSpec((1, D_MODEL), lambda i: (0, 0)),
            pl.BlockSpec((N_EXPERTS, D_MODEL), lambda i: (0, 0)),
        ],
        out_specs=pl.BlockSpec((N_EXPERTS, PROJ_TM), lambda i: (0, i)),
        out_shape=jax.ShapeDtypeStruct((N_EXPERTS, n), F32),
        compiler_params=_cparams(("arbitrary",)),
        name="router",
    )(xy, nw, wr_t)


def _ffn_kernel(idx_ref, idxf_ref, idxp_ref, idxn_ref, gate_ref, nw_ref, wg_ref, wu_ref, wd_ref,
                xy_in_ref, xy_ref, buf, stage, acc_scr, xn_scr, sem_g, sem_s, *,
                tiles_per_expert):
    del xy_in_ref
    tm = FFN_TM
    e = pl.program_id(0)
    i = pl.program_id(1)
    step = e * tiles_per_expert + i
    n_steps = pl.num_programs(0) * tiles_per_expert
    last_tile = i == tiles_per_expert - 1
    y_rows = pl.ds(X_ROWS, X_ROWS)

    def start_gather(ids, dst_slot, lo=0, hi=FFN_TM):
        for r in range(lo, hi):
            pltpu.make_async_copy(xy_ref.at[ids[0, r]], buf.at[dst_slot * tm + r],
                                  sem_g.at[dst_slot]).start()

    def wait_gather(dst_slot):
        pltpu.make_async_copy(xy_ref.at[pl.ds(0, tm)], buf.at[pl.ds(0, tm)],
                              sem_g.at[dst_slot]).wait()

    def start_scatter(ids, src_slot, lo=0, hi=FFN_TM):
        for r in range(lo, hi):
            pltpu.make_async_copy(stage.at[src_slot * tm + r], xy_ref.at[ids[0, r], y_rows],
                                  sem_s.at[0]).start()

    def wait_scatter():
        pltpu.make_async_copy(stage.at[pl.ds(0, tm)], xy_ref.at[pl.ds(0, tm), y_rows],
                              sem_s.at[0]).wait()

    @pl.when(step == 0)
    def _():
        start_gather(idx_ref, 0)

    def run(slot):
        cur = pl.ds(slot * tm, tm)
        prev = pl.ds((1 - slot) * tm, tm)
        wait_gather(slot)

        @pl.when(i == 0)
        def _():
            stage[prev] = buf[cur, X_ROWS:TOK_ROWS, :]

        xn_scr[...] = _rms_rows(_from_tok_rows(buf[cur, 0:X_ROWS, :]), nw_ref[...]).astype(BF16)

        n_f = EXPERT_FF // FFN_FW
        n_pts = 3 * n_f
        cuts = [(tm * k) // n_pts for k in range(n_pts + 1)]

        def issue(k):
            start_scatter(idxp_ref, 1 - slot, cuts[k], cuts[k + 1])
            start_gather(idxf_ref, 1 - slot, cuts[k], cuts[k + 1])

        eye = (lax.broadcasted_iota(jnp.int32, (LANE, LANE), 0)
               == lax.broadcasted_iota(jnp.int32, (LANE, LANE), 1)).astype(F32)
        gate_col = jnp.concatenate(
            [jnp.sum(eye * gate_ref[r:r + 1, :], axis=1, keepdims=True)
             for r in range(tm // LANE)], axis=0)
        for f in range(n_f):
            fs = slice(f * FFN_FW, (f + 1) * FFN_FW)
            issue(3 * f)
            g = jnp.dot(xn_scr[...], wg_ref[:, fs], preferred_element_type=F32)
            issue(3 * f + 1)
            u = jnp.dot(xn_scr[...], wu_ref[:, fs], preferred_element_type=F32)
            hcur = (g * jax.nn.sigmoid(g) * u * gate_col).astype(BF16)
            issue(3 * f + 2)
            part = jnp.dot(hcur, wd_ref[fs, :], preferred_element_type=F32)
            if f == 0:
                acc_scr[...] = part
            elif f == 1:
                acc_scr[...] += part + _from_tok_rows(buf[cur, X_ROWS:TOK_ROWS, :])
            elif f < n_f - 1:
                acc_scr[...] += part
            else:
                wait_scatter()
                stage[cur] = _to_tok_rows(acc_scr[...] + part)

        @pl.when(last_tile)
        def _():
            wait_gather(1 - slot)
            start_scatter(idx_ref, slot)
            wait_scatter()

            @pl.when(step < n_steps - 1)
            def _():
                start_gather(idxn_ref, 1 - slot)

    for slot in (0, 1):
        pl.when(step % 2 == slot)(functools.partial(run, slot))


def _expert_ffn(xy, idx, gate, nw, wg, wu, wd, layer):
    cap = idx.shape[1]
    tm = FFN_TM
    nt = cap // tm
    assert nt >= 2 and EXPERT_FF // FFN_FW >= 3
    n_steps = N_EXPERTS * nt
    idx3 = idx.reshape(n_steps, 1, tm)
    gate3 = gate.reshape(n_steps, tm // LANE, LANE)
    smem_tile = functools.partial(pl.BlockSpec, (None, 1, tm), memory_space=pltpu.SMEM)
    return pl.pallas_call(
        functools.partial(_ffn_kernel, tiles_per_expert=nt),
        grid=(N_EXPERTS, nt),
        in_specs=[
            smem_tile(lambda e, i: (e * nt + i, 0, 0)),
            smem_tile(lambda e, i: (e * nt + jnp.minimum(i + 1, nt - 1), 0, 0)),
            smem_tile(lambda e, i: (e * nt + jnp.maximum(i - 1, 0), 0, 0)),
            smem_tile(lambda e, i: (jnp.minimum(e * nt + i + 1, n_steps - 1), 0, 0)),
            pl.BlockSpec((None, tm // LANE, LANE), lambda e, i: (e * nt + i, 0, 0)),
            pl.BlockSpec((1, D_MODEL), lambda e, i: (0, 0)),
            pl.BlockSpec((None, None, D_MODEL, EXPERT_FF), lambda e, i: (layer, e, 0, 0)),
            pl.BlockSpec((None, None, D_MODEL, EXPERT_FF), lambda e, i: (layer, e, 0, 0)),
            pl.BlockSpec((None, None, EXPERT_FF, D_MODEL), lambda e, i: (layer, e, 0, 0)),
            pl.BlockSpec(memory_space=pl.ANY),
        ],
        out_specs=pl.BlockSpec(memory_space=pl.ANY),
        out_shape=jax.ShapeDtypeStruct(xy.shape, F32),
        scratch_shapes=[
            pltpu.VMEM((2 * tm, TOK_ROWS, LANE), F32),
            pltpu.VMEM((2 * tm, X_ROWS, LANE), F32),
            pltpu.VMEM((tm, D_MODEL), F32),
            pltpu.VMEM((tm, D_MODEL), BF16),
            pltpu.SemaphoreType.DMA((2,)),
            pltpu.SemaphoreType.DMA((1,)),
        ],
        input_output_aliases={9: 0},
        compiler_params=_cparams(("arbitrary", "arbitrary")),
        name="expert_ffn",
    )(idx3, idx3, idx3, idx3, gate3, nw, wg, wu, wd, xy)


def _threshold_kernel(aff_ref, thr_ref, need_ref, *, k):
    bits = lax.bitcast_convert_type(aff_ref[...], I32)

    def body(b, t):
        cand = t | (jnp.int32(1) << (30 - b))
        cnt = jnp.sum((bits >= cand).astype(I32), axis=1, keepdims=True)
        return jnp.where(cnt >= k, cand, t)

    t = lax.fori_loop(0, 31, body, jnp.zeros((N_EXPERTS, 1), I32))
    n_gt = jnp.sum((bits > t).astype(I32), axis=1, keepdims=True)
    thr_ref[...] = jnp.broadcast_to(lax.bitcast_convert_type(t, F32), thr_ref.shape)
    need_ref[...] = jnp.broadcast_to(k - n_gt, need_ref.shape)


def _threshold(aff, k):
    return pl.pallas_call(
        functools.partial(_threshold_kernel, k=k),
        out_shape=[jax.ShapeDtypeStruct((N_EXPERTS, LANE), F32),
                   jax.ShapeDtypeStruct((N_EXPERTS, LANE), I32)],
        compiler_params=pltpu.CompilerParams(vmem_limit_bytes=VMEM_LIMIT_BYTES),
        name="topk_threshold",
    )(aff)


def _compact_sc(aff_flat, thr_flat, need_flat, n, cap):
    lanes = SC_LANES
    mesh = plsc.VectorSubcoreMesh(core_axis_name="c", subcore_axis_name="s", num_cores=1,
                                  num_subcores=N_EXPERTS)

    @functools.partial(
        pl.kernel,
        out_type=(jax.ShapeDtypeStruct((N_EXPERTS * cap,), I32),
                  jax.ShapeDtypeStruct((N_EXPERTS * cap,), F32)),
        mesh=mesh,
        scratch_types=[pltpu.VMEM((n,), F32), pltpu.VMEM((cap,), I32), pltpu.VMEM((cap,), F32),
                       pltpu.VMEM((lanes,), F32), pltpu.VMEM((lanes,), I32)],
        compiler_params=pltpu.CompilerParams(needs_layout_passes=False),
        name="topk_compact",
    )
    def compact(aff_hbm, thr_hbm, need_hbm, idx_hbm, gate_hbm, row, idx_v, gate_v, thr_v, need_v):
        e = lax.axis_index("s")
        pltpu.sync_copy(aff_hbm.at[pl.ds(e * n, n)], row)
        pltpu.sync_copy(thr_hbm.at[pl.ds(e * LANE, lanes)], thr_v)
        pltpu.sync_copy(need_hbm.at[pl.ds(e * LANE, lanes)], need_v)
        thr = thr_v[...]
        need = need_v[...]
        lane = lax.iota(I32, lanes)
        zero_i = jnp.zeros((lanes,), I32)

        def init(i, carry):
            idx_v[pl.ds(i * lanes, lanes)] = zero_i
            gate_v[pl.ds(i * lanes, lanes)] = jnp.zeros((lanes,), F32)
            return carry

        lax.fori_loop(0, cap // lanes, init, 0)

        def body(i, carry):
            off, eqs = carry
            v = row[pl.ds(i * lanes, lanes)]
            m_gt = v > thr
            m_eq = v == thr
            eq_rank = plsc.cumsum(m_eq.astype(I32)) + eqs
            take = m_gt | (m_eq & (eq_rank <= need))
            pos = off + plsc.cumsum(take.astype(I32)) - 1
            take = take & (pos < cap)
            plsc.store_scatter(idx_v, [pos], lane + i * lanes, mask=take)
            plsc.store_scatter(gate_v, [pos], v, mask=take)
            off = off + plsc.all_reduce_population_count(take)
            eqs = eqs + plsc.all_reduce_population_count(m_eq)
            return off, eqs

        lax.fori_loop(0, n // lanes, body, (zero_i, zero_i))
        pltpu.sync_copy(idx_v, idx_hbm.at[pl.ds(e * cap, cap)])
        pltpu.sync_copy(gate_v, gate_hbm.at[pl.ds(e * cap, cap)])

    return compact(aff_flat, thr_flat, need_flat)


def _expert_choice(aff, cap):
    n = aff.shape[1]
    thr, need = _threshold(aff, cap)
    idx, gate = _compact_sc(aff.reshape(-1), thr.reshape(-1), need.reshape(-1), n, cap)
    return gate.reshape(N_EXPERTS, cap), idx.reshape(N_EXPERTS, cap)


def _moe(xy, p, layer):
    n = xy.shape[0]
    cap = EC_CAPACITY_FACTOR * n // N_EXPERTS
    nw = p["norm_ffn"][layer]
    aff = _router(xy, nw, p["w_router_t"][layer])
    gate, idx = _expert_choice(aff, cap)
    return _expert_ffn(xy, idx, gate, nw, p["w_gate"], p["w_up"], p["w_down"], layer)


def _final_norm_kernel(xy_ref, nw_ref, o_ref):
    o_ref[...] = _rms_rows(_from_tok_rows(xy_ref[...]), nw_ref[...])


def _final_norm(xy, nw):
    n = xy.shape[0]
    return pl.pallas_call(
        _final_norm_kernel,
        grid=(n // PROJ_TM,),
        in_specs=[pl.BlockSpec((PROJ_TM, X_ROWS, LANE), lambda i: (i, 1, 0)),
                  pl.BlockSpec((1, D_MODEL), lambda i: (0, 0))],
        out_specs=pl.BlockSpec((PROJ_TM, D_MODEL), lambda i: (i, 0)),
        out_shape=jax.ShapeDtypeStruct((n, D_MODEL), F32),
        compiler_params=_cparams(("arbitrary",)),
        name="final_norm",
    )(xy, nw)


def _rope_tables(seq):
    half = RET_DK // 2
    inv = 1.0 / (ROPE_BASE ** (jnp.arange(half, dtype=F32) / half))
    ang = jnp.arange(seq, dtype=F32)[:, None] * inv[None, :]
    return jnp.cos(ang), jnp.sin(ang)


def _trunk(x, p):
    b, t, d = x.shape
    n = b * t
    cos, sin = _rope_tables(t)

    proj = _ret_inproj(x.reshape(n, d), 0, p["norm_mix"][0], p["ret_w_in"], cos, sin, t)
    xy = _retention(proj.reshape(b, t, -1), x, p["ret_decay"], p["ret_w_out"])
    xy = _moe(xy.reshape(n, TOK_ROWS, LANE), p, 0)

    a, lf = _hgrn_inproj(xy, p["norm_mix"][1], p["hgrn_w_in"], p["hgrn_lb"])
    xy = _hgrn(a.reshape(b, t, -1), lf.reshape(b, t, -1), xy.reshape(b, t, TOK_ROWS, LANE),
               p["hgrn_norm"], p["hgrn_w_out"])
    xy = _moe(xy.reshape(n, TOK_ROWS, LANE), p, 1)

    return _final_norm(xy, p["norm_final"]).reshape(b, t, d)


def kernel(x_prompt, x_sample, norm_mix_w, norm_ffn_w, norm_final_w, ret_w_in, ret_w_out,
           hgrn_w_in, hgrn_lb, hgrn_norm_w, hgrn_w_out, moe_w_router, moe_w_gate, moe_w_up,
           moe_w_down):
    depth = norm_mix_w.shape[0]
    assert depth == 2 and ret_w_in.shape[0] == 1 and hgrn_w_in.shape[0] == 1
    sm = jax.nn.softmax(hgrn_lb.astype(F32), axis=0)
    lower_bounds = jnp.cumsum(sm, axis=0) - sm[0:1]
    p = {
        "norm_mix": norm_mix_w.reshape(depth, 1, D_MODEL),
        "norm_ffn": norm_ffn_w.reshape(depth, 1, D_MODEL),
        "norm_final": norm_final_w.reshape(1, D_MODEL),
        "ret_w_in": ret_w_in[0].astype(BF16),
        "ret_w_out": ret_w_out[0].astype(BF16),
        "ret_decay": _ret_decay_matrix(),
        "hgrn_w_in": hgrn_w_in[0].astype(BF16),
        "hgrn_lb": lower_bounds[1].reshape(1, 2 * D_MODEL),
        "hgrn_norm": hgrn_norm_w[0].reshape(1, D_MODEL),
        "hgrn_w_out": hgrn_w_out[0].astype(BF16),
        "w_router_t": jnp.swapaxes(moe_w_router, 1, 2),
        "w_gate": moe_w_gate.astype(BF16),
        "w_up": moe_w_up.astype(BF16),
        "w_down": moe_w_down.astype(BF16),
    }
    return _trunk(x_prompt, p), _trunk(x_sample, p)
```

```python
import functools
import math

import jax
import jax.numpy as jnp
from jax import lax
from jax.experimental import pallas as pl
from jax.experimental.pallas import tpu as pltpu
from jax.experimental.pallas import tpu_sc as plsc

F32 = jnp.float32
I32 = jnp.int32
BF16 = jnp.bfloat16

D_MODEL = 1024
NORM_EPS = 1e-6
ROPE_BASE = 10000.0

RET_HEADS = 4
RET_DK = 256
RET_DV = 512
RET_QK_WIDTH = RET_HEADS * RET_DK
RET_V_WIDTH = RET_HEADS * RET_DV
RET_DECAY_FWD = 5.0
RET_DECAY_BWD = 5.5
RET_BLOCK = 512

HGRN_HEADS = 8
HGRN_DK = 128
HGRN_CHUNK = 64
HGRN_BLOCK = 256

N_EXPERTS = 16
EC_CAPACITY_FACTOR = 2
EXPERT_FF = 2 * D_MODEL

PROJ_TM = 512
PROJ_CW = 512
FFN_TM = 512
FFN_FW = 512
LANE = 128
SC_LANES = 16
X_ROWS = D_MODEL // LANE
TOK_ROWS = 2 * X_ROWS

VMEM_LIMIT_BYTES = 52 * 1024 * 1024


def _cparams(sem):
    return pltpu.CompilerParams(dimension_semantics=sem, vmem_limit_bytes=VMEM_LIMIT_BYTES)


def _ret_log_gamma(offset):
    return [math.log1p(-(2.0 ** (-offset - h))) for h in range(RET_HEADS)]


def _rms_rows(x, w):
    ms = jnp.mean(x * x, axis=-1, keepdims=True)
    return x * lax.rsqrt(ms + NORM_EPS) * w


def _sigmoid(x):
    return 0.5 * jnp.tanh(0.5 * x) + 0.5


def _silu(x):
    h = 0.5 * x
    return h + h * jnp.tanh(h)


def _from_tok_rows(v):
    return v.reshape(v.shape[0], D_MODEL)


def _to_tok_rows(v):
    return v.reshape(v.shape[0], X_ROWS, LANE)


def _store_xy(xy_ref, val):
    r = _to_tok_rows(val)
    xy_ref[:, 0:X_ROWS, :] = r
    xy_ref[:, X_ROWS:TOK_ROWS, :] = r


def _ret_inproj_kernel(x_ref, nw_ref, w_ref, cos_ref, sin_ref, o_ref, xn_scr):
    xn_scr[...] = _rms_rows(x_ref[...], nw_ref[...]).astype(BF16)
    cos = cos_ref[...]
    sin = sin_ref[...]
    width = w_ref.shape[1]
    half = RET_DK // 2
    for j in range(width // PROJ_CW):
        c0 = j * PROJ_CW
        r = jnp.dot(xn_scr[...], w_ref[:, c0:c0 + PROJ_CW], preferred_element_type=F32)
        if c0 < 2 * RET_QK_WIDTH:
            scale = 1.0 if c0 < RET_QK_WIDTH else RET_DK ** -0.5
            parts = []
            for hh in range(PROJ_CW // RET_DK):
                x1 = r[:, hh * RET_DK:hh * RET_DK + half]
                x2 = r[:, hh * RET_DK + half:(hh + 1) * RET_DK]
                parts.append((x1 * cos - x2 * sin) * scale)
                parts.append((x1 * sin + x2 * cos) * scale)
            r = jnp.concatenate(parts, axis=-1)
        elif c0 >= 2 * RET_QK_WIDTH + RET_V_WIDTH:
            r = _silu(r)
        o_ref[:, c0:c0 + PROJ_CW] = r.astype(BF16)


def _ret_inproj(x2d, col, nw, w, cos, sin, seq):
    n = x2d.shape[0]
    width = w.shape[1]
    tpb = seq // PROJ_TM
    return pl.pallas_call(
        _ret_inproj_kernel,
        grid=(n // PROJ_TM,),
        in_specs=[
            pl.BlockSpec((PROJ_TM, D_MODEL), lambda i: (i, col)),
            pl.BlockSpec((1, D_MODEL), lambda i: (0, 0)),
            pl.BlockSpec((D_MODEL, width), lambda i: (0, 0)),
            pl.BlockSpec((PROJ_TM, RET_DK // 2), lambda i: (i % tpb, 0)),
            pl.BlockSpec((PROJ_TM, RET_DK // 2), lambda i: (i % tpb, 0)),
        ],
        out_specs=pl.BlockSpec((PROJ_TM, width), lambda i: (i, 0)),
        out_shape=jax.ShapeDtypeStruct((n, width), BF16),
        scratch_shapes=[pltpu.VMEM((PROJ_TM, D_MODEL), BF16)],
        compiler_params=_cparams(("arbitrary",)),
        name="ret_inproj",
    )(x2d, nw, w, cos, sin)


def _hgrn_inproj_kernel(xy_ref, nw_ref, w_ref, lb_ref, a_ref, l_ref, xn_scr):
    xn_scr[...] = _rms_rows(_from_tok_rows(xy_ref[...]), nw_ref[...]).astype(BF16)
    width = w_ref.shape[1]
    for j in range(width // PROJ_CW):
        c0 = j * PROJ_CW
        r = jnp.dot(xn_scr[...], w_ref[:, c0:c0 + PROJ_CW], preferred_element_type=F32)
        if c0 < D_MODEL:
            a_ref[:, c0:c0 + PROJ_CW] = _silu(r).astype(BF16)
        elif c0 < 2 * D_MODEL:
            a_ref[:, c0:c0 + PROJ_CW] = r.astype(BF16)
        elif c0 < 4 * D_MODEL:
            g0 = c0 - 2 * D_MODEL
            lb = lb_ref[:, g0:g0 + PROJ_CW]
            f = 0.5 * (1.0 + lb) + (0.5 * (1.0 - lb)) * jnp.tanh(0.5 * r)
            a_ref[:, c0:c0 + PROJ_CW] = (1.0 - f).astype(BF16)
            l_ref[:, g0:g0 + PROJ_CW] = jnp.log(f)
        else:
            a_ref[:, c0:c0 + PROJ_CW] = _sigmoid(r).astype(BF16)


def _hgrn_inproj(xy, nw, w, lb):
    n = xy.shape[0]
    width = w.shape[1]
    return pl.pallas_call(
        _hgrn_inproj_kernel,
        grid=(n // PROJ_TM,),
        in_specs=[
            pl.BlockSpec((PROJ_TM, X_ROWS, LANE), lambda i: (i, 1, 0)),
            pl.BlockSpec((1, D_MODEL), lambda i: (0, 0)),
            pl.BlockSpec((D_MODEL, width), lambda i: (0, 0)),
            pl.BlockSpec((1, 2 * D_MODEL), lambda i: (0, 0)),
        ],
        out_specs=[
            pl.BlockSpec((PROJ_TM, width), lambda i: (i, 0)),
            pl.BlockSpec((PROJ_TM, 2 * D_MODEL), lambda i: (i, 0)),
        ],
        out_shape=[
            jax.ShapeDtypeStruct((n, width), BF16),
            jax.ShapeDtypeStruct((n, 2 * D_MODEL), F32),
        ],
        scratch_shapes=[pltpu.VMEM((PROJ_TM, D_MODEL), BF16)],
        compiler_params=_cparams(("arbitrary",)),
        name="hgrn_inproj",
    )(xy, nw, w, lb)


_NT = (((1,), (1,)), ((), ()))
_TN = (((0,), (0,)), ((), ()))


def _ret_bwd_kernel(q_ref, k_ref, v_ref, ob_ref, r_scr, *, lg):
    @pl.when(pl.program_id(1) == 0)
    def _():
        r_scr[...] = jnp.zeros_like(r_scr)

    c = q_ref.shape[0]
    pos = lax.broadcasted_iota(jnp.int32, (c, 1), 0).astype(F32)
    for h in range(RET_HEADS):
        qh = q_ref[:, h * RET_DK:(h + 1) * RET_DK].astype(F32)
        kh = k_ref[:, h * RET_DK:(h + 1) * RET_DK].astype(F32)
        vh = v_ref[:, h * RET_DV:(h + 1) * RET_DV]
        qd = (qh * jnp.exp((c - pos) * lg[h])).astype(BF16)
        o = jnp.dot(qd, r_scr[h].astype(BF16), preferred_element_type=F32)
        ob_ref[:, h * RET_DV:(h + 1) * RET_DV] = o.astype(BF16)
        kd = (kh * jnp.exp(pos * lg[h])).astype(BF16)
        r_scr[h] = r_scr[h] * math.exp(c * lg[h]) + lax.dot_general(
            kd, vh, _TN, preferred_element_type=F32)


def _ret_fwd_kernel(q_ref, k_ref, v_ref, g_ref, ob_ref, x_ref, m_ref, wo_ref, xy_ref,
                    r_scr, *, lg):
    @pl.when(pl.program_id(1) == 0)
    def _():
        r_scr[...] = jnp.zeros_like(r_scr)

    c = q_ref.shape[0]
    pos = lax.broadcasted_iota(jnp.int32, (c, 1), 0).astype(F32)
    acc = x_ref[...]
    for h in range(RET_HEADS):
        qb = q_ref[:, h * RET_DK:(h + 1) * RET_DK]
        kb = k_ref[:, h * RET_DK:(h + 1) * RET_DK]
        vh = v_ref[:, h * RET_DV:(h + 1) * RET_DV]
        s = lax.dot_general(qb, kb, _NT, preferred_element_type=F32)
        sm = (s * m_ref[h]).astype(BF16)
        qd = (qb.astype(F32) * jnp.exp((pos + 1.0) * lg[h])).astype(BF16)
        o = (jnp.dot(sm, vh, preferred_element_type=F32)
             + jnp.dot(qd, r_scr[h].astype(BF16), preferred_element_type=F32)
             + ob_ref[:, h * RET_DV:(h + 1) * RET_DV].astype(F32))
        kd = (kb.astype(F32) * jnp.exp((c - 1.0 - pos) * lg[h])).astype(BF16)
        r_scr[h] = r_scr[h] * math.exp(c * lg[h]) + lax.dot_general(
            kd, vh, _TN, preferred_element_type=F32)
        ms = jnp.mean(o * o, axis=-1, keepdims=True)
        gated = (o * lax.rsqrt(ms + NORM_EPS)
                 * g_ref[:, h * RET_DV:(h + 1) * RET_DV].astype(F32)).astype(BF16)
        acc = acc + jnp.dot(gated, wo_ref[h * RET_DV:(h + 1) * RET_DV, :],
                            preferred_element_type=F32)
    _store_xy(xy_ref, acc)


def _retention(proj, x3, decay_mat, w_out):
    b, t, _ = proj.shape
    c = RET_BLOCK
    nc = t // c
    lgf = _ret_log_gamma(RET_DECAY_FWD)
    lgb = _ret_log_gamma(RET_DECAY_BWD)
    qk = RET_QK_WIDTH
    vw = RET_V_WIDTH
    state = pltpu.VMEM((RET_HEADS, RET_DK, RET_DV), F32)

    ob = pl.pallas_call(
        functools.partial(_ret_bwd_kernel, lg=lgb),
        grid=(b, nc),
        in_specs=[
            pl.BlockSpec((None, c, qk), lambda bi, i: (bi, nc - 1 - i, 0)),
            pl.BlockSpec((None, c, qk), lambda bi, i: (bi, nc - 1 - i, 1)),
            pl.BlockSpec((None, c, vw), lambda bi, i: (bi, nc - 1 - i, 1)),
        ],
        out_specs=pl.BlockSpec((None, c, vw), lambda bi, i: (bi, nc - 1 - i, 0)),
        out_shape=jax.ShapeDtypeStruct((b, t, vw), BF16),
        scratch_shapes=[state],
        compiler_params=_cparams(("arbitrary", "arbitrary")),
        name="ret_bwd",
    )(proj, proj, proj)

    return pl.pallas_call(
        functools.partial(_ret_fwd_kernel, lg=lgf),
        grid=(b, nc),
        in_specs=[
            pl.BlockSpec((None, c, qk), lambda bi, i: (bi, i, 0)),
            pl.BlockSpec((None, c, qk), lambda bi, i: (bi, i, 1)),
            pl.BlockSpec((None, c, vw), lambda bi, i: (bi, i, 1)),
            pl.BlockSpec((None, c, vw), lambda bi, i: (bi, i, 2)),
            pl.BlockSpec((None, c, vw), lambda bi, i: (bi, i, 0)),
            pl.BlockSpec((None, c, D_MODEL), lambda bi, i: (bi, i, 0)),
            pl.BlockSpec((RET_HEADS, c, c), lambda bi, i: (0, 0, 0)),
            pl.BlockSpec((vw, D_MODEL), lambda bi, i: (0, 0)),
        ],
        out_specs=pl.BlockSpec((None, c, TOK_ROWS, LANE), lambda bi, i: (bi, i, 0, 0)),
        out_shape=jax.ShapeDtypeStruct((b, t, TOK_ROWS, LANE), F32),
        scratch_shapes=[state],
        compiler_params=_cparams(("arbitrary", "arbitrary")),
        name="ret_fwd",
    )(proj, proj, proj, proj, ob, x3, decay_mat, w_out)


def _ret_decay_matrix():
    c = RET_BLOCK
    rel = (jnp.arange(c, dtype=F32)[:, None] - jnp.arange(c, dtype=F32)[None, :])[None]
    lgf = jnp.asarray(_ret_log_gamma(RET_DECAY_FWD), F32)[:, None, None]
    lgb = jnp.asarray(_ret_log_gamma(RET_DECAY_BWD), F32)[:, None, None]
    return jnp.where(rel >= 0, jnp.exp(jnp.maximum(rel, 0.0) * lgf),
                     jnp.exp(jnp.maximum(-rel, 0.0) * lgb))


def _tri_cumsum(tri, x):
    hi = x.astype(BF16)
    lo = (x - hi.astype(F32)).astype(BF16)
    return (jnp.dot(tri, hi, preferred_element_type=F32)
            + jnp.dot(tri, lo, preferred_element_type=F32))


def _hgrn_block(q_ref, k_ref, v_ref, lf_ref, s_scr, reverse):
    tb = HGRN_BLOCK
    c = HGRN_CHUNK
    n_sub = tb // c
    shift = c.bit_length() - 1
    row = lax.broadcasted_iota(jnp.int32, (tb, tb), 0)
    col = lax.broadcasted_iota(jnp.int32, (tb, tb), 1)
    same_chunk = lax.shift_right_logical(row, shift) == lax.shift_right_logical(col, shift)
    keep = same_chunk & ((col >= row) if reverse else (col <= row))
    g = _tri_cumsum(keep.astype(BF16), lf_ref[...])
    g_tot = jnp.concatenate(
        [jnp.broadcast_to(g[i * c:i * c + 1, :] if reverse else g[(i + 1) * c - 1:(i + 1) * c, :],
                          (c, g.shape[1])) for i in range(n_sub)], axis=0)
    qg = (q_ref[...].astype(F32) * jnp.exp(g)).astype(BF16)
    kf = k_ref[...].astype(F32)
    kg = (kf * jnp.exp(-g)).astype(BF16)
    kd = (kf * jnp.exp(g_tot - g)).astype(BF16)
    v = v_ref[...]
    order = list(reversed(range(n_sub))) if reverse else list(range(n_sub))
    e_tot = [jnp.exp(g_tot[i * c:i * c + 1, :]) for i in range(n_sub)]
    outs = []
    for h in range(HGRN_HEADS):
        cs = slice(h * HGRN_DK, (h + 1) * HGRN_DK)
        s = lax.dot_general(qg[:, cs], kg[:, cs], _NT, preferred_element_type=F32)
        p = jnp.where(keep, s, 0.0).astype(BF16)
        o_h = jnp.dot(p, v[:, cs], preferred_element_type=F32)
        st = s_scr[h]
        pieces = [None] * n_sub
        for i in order:
            rows = slice(i * c, (i + 1) * c)
            pieces[i] = o_h[rows, :] + lax.dot_general(
                qg[rows, cs], st.astype(BF16), _NT, preferred_element_type=F32)
            st = st * e_tot[i][:, cs] + lax.dot_general(
                v[rows, cs], kd[rows, cs], _TN, preferred_element_type=F32)
        s_scr[h] = st
        outs.append(jnp.concatenate(pieces, axis=0))
    return jnp.concatenate(outs, axis=-1)


def _hgrn_bwd_kernel(q_ref, v_ref, k_ref, lf_ref, ob_ref, s_scr):
    @pl.when(pl.program_id(1) == 0)
    def _():
        s_scr[...] = jnp.zeros_like(s_scr)

    ob_ref[...] = _hgrn_block(q_ref, k_ref, v_ref, lf_ref, s_scr, reverse=True).astype(BF16)


def _hgrn_fwd_kernel(q_ref, v_ref, k_ref, lf_ref, sg_ref, ob_ref, x_ref, nw_ref, wo_ref,
                     xy_ref, s_scr, o_scr):
    @pl.when(pl.program_id(1) == 0)
    def _():
        s_scr[...] = jnp.zeros_like(s_scr)

    o_scr[...] = (_hgrn_block(q_ref, k_ref, v_ref, lf_ref, s_scr, reverse=False)
                  + ob_ref[...].astype(F32))
    on = _rms_rows(o_scr[...], nw_ref[...])
    gated = (on * sg_ref[...].astype(F32)).astype(BF16)
    xn = (_from_tok_rows(x_ref[...])
          + jnp.dot(gated, wo_ref[...], preferred_element_type=F32))
    _store_xy(xy_ref, xn)


def _hgrn(a3, l3, xy3, norm_w, w_out):
    b, t, _ = a3.shape
    tb = HGRN_BLOCK
    nb = t // tb
    d = D_MODEL
    state = pltpu.VMEM((HGRN_HEADS, HGRN_DK, HGRN_DK), F32)

    ob = pl.pallas_call(
        _hgrn_bwd_kernel,
        grid=(b, nb),
        in_specs=[
            pl.BlockSpec((None, tb, d), lambda bi, i: (bi, nb - 1 - i, 0)),
            pl.BlockSpec((None, tb, d), lambda bi, i: (bi, nb - 1 - i, 1)),
            pl.BlockSpec((None, tb, d), lambda bi, i: (bi, nb - 1 - i, 3)),
            pl.BlockSpec((None, tb, d), lambda bi, i: (bi, nb - 1 - i, 1)),
        ],
        out_specs=pl.BlockSpec((None, tb, d), lambda bi, i: (bi, nb - 1 - i, 0)),
        out_shape=jax.ShapeDtypeStruct((b, t, d), BF16),
        scratch_shapes=[state],
        compiler_params=_cparams(("arbitrary", "arbitrary")),
        name="hgrn_bwd",
    )(a3, a3, a3, l3)

    return pl.pallas_call(
        _hgrn_fwd_kernel,
        grid=(b, nb),
        in_specs=[
            pl.BlockSpec((None, tb, d), lambda bi, i: (bi, i, 0)),
            pl.BlockSpec((None, tb, d), lambda bi, i: (bi, i, 1)),
            pl.BlockSpec((None, tb, d), lambda bi, i: (bi, i, 2)),
            pl.BlockSpec((None, tb, d), lambda bi, i: (bi, i, 0)),
            pl.BlockSpec((None, tb, d), lambda bi, i: (bi, i, 4)),
            pl.BlockSpec((None, tb, d), lambda bi, i: (bi, i, 0)),
            pl.BlockSpec((None, tb, X_ROWS, LANE), lambda bi, i: (bi, i, 1, 0)),
            pl.BlockSpec((1, d), lambda bi, i: (0, 0)),
            pl.BlockSpec((d, d), lambda bi, i: (0, 0)),
        ],
        out_specs=pl.BlockSpec((None, tb, TOK_ROWS, LANE), lambda bi, i: (bi, i, 0, 0)),
        out_shape=jax.ShapeDtypeStruct((b, t, TOK_ROWS, LANE), F32),
        scratch_shapes=[state, pltpu.VMEM((tb, d), F32)],
        compiler_params=_cparams(("arbitrary", "arbitrary")),
        name="hgrn_fwd",
    )(a3, a3, a3, l3, a3, ob, xy3, norm_w, w_out)


def _router_kernel(xy_ref, nw_ref, wr_ref, aff_ref):
    xn = _rms_rows(_from_tok_rows(xy_ref[...]), nw_ref[...])
    xh = xn.astype(BF16)
    xl = (xn - xh.astype(F32)).astype(BF16)
    wr = wr_ref[...]
    wh = wr.astype(BF16)
    wl = (wr - wh.astype(F32)).astype(BF16)
    logits = (lax.dot_general(wh, xh, _NT, preferred_element_type=F32)
              + lax.dot_general(wh, xl, _NT, preferred_element_type=F32)
              + lax.dot_general(wl, xh, _NT, preferred_element_type=F32))
    m = jnp.max(logits, axis=0, keepdims=True)
    e = jnp.exp(logits - m)
    aff_ref[...] = e / jnp.sum(e, axis=0, keepdims=True)


def _router(xy, nw, wr_t):
    n = xy.shape[0]
    return pl.pallas_call(
        _router_kernel,
        grid=(n // PROJ_TM,),
        in_specs=[
            pl.BlockSpec((PROJ_TM, X_ROWS, LANE), lambda i: (i, 0, 0)),
            pl.BlockSpec((1, D_MODEL), lambda i: (0, 0)),
            pl.BlockSpec((N_EXPERTS, D_MODEL), lambda i: (0, 0)),
        ],
        out_specs=pl.BlockSpec((N_EXPERTS, PROJ_TM), lambda i: (0, i)),
        out_shape=jax.ShapeDtypeStruct((N_EXPERTS, n), F32),
        compiler_params=_cparams(("arbitrary",)),
        name="router",
    )(xy, nw, wr_t)


def _ffn_kernel(idx_ref, idxf_ref, idxp_ref, idxn_ref, gate_ref, nw_ref, wg_ref, wu_ref, wd_ref,
                xy_in_ref, xy_ref, buf, stage, acc_scr, xn_scr, sem_g, sem_s, *,
                tiles_per_expert):
    del xy_in_ref
    tm = FFN_TM
    e = pl.program_id(0)
    i = pl.program_id(1)
    step = e * tiles_per_expert + i
    n_steps = pl.num_programs(0) * tiles_per_expert
    last_tile = i == tiles_per_expert - 1
    y_rows = pl.ds(X_ROWS, X_ROWS)

    def start_gather(ids, dst_slot, lo=0, hi=FFN_TM):
        for r in range(lo, hi):
            pltpu.make_async_copy(xy_ref.at[ids[0, r]], buf.at[dst_slot * tm + r],
                                  sem_g.at[dst_slot]).start()

    def wait_gather(dst_slot):
        pltpu.make_async_copy(xy_ref.at[pl.ds(0, tm)], buf.at[pl.ds(0, tm)],
                              sem_g.at[dst_slot]).wait()

    def start_scatter(ids, src_slot, lo=0, hi=FFN_TM):
        for r in range(lo, hi):
            pltpu.make_async_copy(stage.at[src_slot * tm + r], xy_ref.at[ids[0, r], y_rows],
                                  sem_s.at[0]).start()

    def wait_scatter():
        pltpu.make_async_copy(stage.at[pl.ds(0, tm)], xy_ref.at[pl.ds(0, tm), y_rows],
                              sem_s.at[0]).wait()

    @pl.when(step == 0)
    def _():
        start_gather(idx_ref, 0)

    def run(slot):
        cur = pl.ds(slot * tm, tm)
        prev = pl.ds((1 - slot) * tm, tm)
        wait_gather(slot)

        @pl.when(i == 0)
        def _():
            stage[prev] = buf[cur, X_ROWS:TOK_ROWS, :]

        xn_scr[...] = _rms_rows(_from_tok_rows(buf[cur, 0:X_ROWS, :]), nw_ref[...]).astype(BF16)

        n_f = EXPERT_FF // FFN_FW
        n_pts = 3 * n_f
        cuts = [(tm * k) // n_pts for k in range(n_pts + 1)]

        def issue(k):
            start_scatter(idxp_ref, 1 - slot, cuts[k], cuts[k + 1])
            start_gather(idxf_ref, 1 - slot, cuts[k], cuts[k + 1])

        eye = (lax.broadcasted_iota(jnp.int32, (LANE, LANE), 0)
               == lax.broadcasted_iota(jnp.int32, (LANE, LANE), 1)).astype(F32)
        gate_col = jnp.concatenate(
            [jnp.sum(eye * gate_ref[r:r + 1, :], axis=1, keepdims=True)
             for r in range(tm // LANE)], axis=0)
        for f in range(n_f):
            fs = slice(f * FFN_FW, (f + 1) * FFN_FW)
            issue(3 * f)
            g = jnp.dot(xn_scr[...], wg_ref[:, fs], preferred_element_type=F32)
            issue(3 * f + 1)
            u = jnp.dot(xn_scr[...], wu_ref[:, fs], preferred_element_type=F32)
            hcur = (g * _sigmoid(g) * u * gate_col).astype(BF16)
            issue(3 * f + 2)
            part = jnp.dot(hcur, wd_ref[fs, :], preferred_element_type=F32)
            if f == 0:
                acc_scr[...] = part
            elif f == 1:
                acc_scr[...] += part + _from_tok_rows(buf[cur, X_ROWS:TOK_ROWS, :])
            elif f < n_f - 1:
                acc_scr[...] += part
            else:
                wait_scatter()
                stage[cur] = _to_tok_rows(acc_scr[...] + part)

        @pl.when(last_tile)
        def _():
            wait_gather(1 - slot)
            start_scatter(idx_ref, slot)
            wait_scatter()

            @pl.when(step < n_steps - 1)
            def _():
                start_gather(idxn_ref, 1 - slot)

    for slot in (0, 1):
        pl.when(step % 2 == slot)(functools.partial(run, slot))


def _expert_ffn(xy, idx, gate, nw, wg, wu, wd, layer):
    cap = idx.shape[1]
    tm = FFN_TM
    nt = cap // tm
    assert nt >= 2 and EXPERT_FF // FFN_FW >= 3
    n_steps = N_EXPERTS * nt
    idx3 = idx.reshape(n_steps, 1, tm)
    gate3 = gate.reshape(n_steps, tm // LANE, LANE)
    smem_tile = functools.partial(pl.BlockSpec, (None, 1, tm), memory_space=pltpu.SMEM)
    return pl.pallas_call(
        functools.partial(_ffn_kernel, tiles_per_expert=nt),
        grid=(N_EXPERTS, nt),
        in_specs=[
            smem_tile(lambda e, i: (e * nt + i, 0, 0)),
            smem_tile(lambda e, i: (e * nt + jnp.minimum(i + 1, nt - 1), 0, 0)),
            smem_tile(lambda e, i: (e * nt + jnp.maximum(i - 1, 0), 0, 0)),
            smem_tile(lambda e, i: (jnp.minimum(e * nt + i + 1, n_steps - 1), 0, 0)),
            pl.BlockSpec((None, tm // LANE, LANE), lambda e, i: (e * nt + i, 0, 0)),
            pl.BlockSpec((1, D_MODEL), lambda e, i: (0, 0)),
            pl.BlockSpec((None, None, D_MODEL, EXPERT_FF), lambda e, i: (layer, e, 0, 0)),
            pl.BlockSpec((None, None, D_MODEL, EXPERT_FF), lambda e, i: (layer, e, 0, 0)),
            pl.BlockSpec((None, None, EXPERT_FF, D_MODEL), lambda e, i: (layer, e, 0, 0)),
            pl.BlockSpec(memory_space=pl.ANY),
        ],
        out_specs=pl.BlockSpec(memory_space=pl.ANY),
        out_shape=jax.ShapeDtypeStruct(xy.shape, F32),
        scratch_shapes=[
            pltpu.VMEM((2 * tm, TOK_ROWS, LANE), F32),
            pltpu.VMEM((2 * tm, X_ROWS, LANE), F32),
            pltpu.VMEM((tm, D_MODEL), F32),
            pltpu.VMEM((tm, D_MODEL), BF16),
            pltpu.SemaphoreType.DMA((2,)),
            pltpu.SemaphoreType.DMA((1,)),
        ],
        input_output_aliases={9: 0},
        compiler_params=_cparams(("arbitrary", "arbitrary")),
        name="expert_ffn",
    )(idx3, idx3, idx3, idx3, gate3, nw, wg, wu, wd, xy)


def _threshold_kernel(aff_ref, thr_ref, need_ref, *, k):
    bits = lax.bitcast_convert_type(aff_ref[...], I32)

    def body(b, t):
        cand = t | (jnp.int32(1) << (30 - b))
        cnt = jnp.sum((bits >= cand).astype(I32), axis=1, keepdims=True)
        return jnp.where(cnt >= k, cand, t)

    t = lax.fori_loop(0, 31, body, jnp.zeros((N_EXPERTS, 1), I32))
    n_gt = jnp.sum((bits > t).astype(I32), axis=1, keepdims=True)
    thr_ref[...] = jnp.broadcast_to(lax.bitcast_convert_type(t, F32), thr_ref.shape)
    need_ref[...] = jnp.broadcast_to(k - n_gt, need_ref.shape)


def _threshold(aff, k):
    return pl.pallas_call(
        functools.partial(_threshold_kernel, k=k),
        out_shape=[jax.ShapeDtypeStruct((N_EXPERTS, LANE), F32),
                   jax.ShapeDtypeStruct((N_EXPERTS, LANE), I32)],
        compiler_params=pltpu.CompilerParams(vmem_limit_bytes=VMEM_LIMIT_BYTES),
        name="topk_threshold",
    )(aff)


def _compact_sc(aff_flat, thr_flat, need_flat, n, cap):
    lanes = SC_LANES
    mesh = plsc.VectorSubcoreMesh(core_axis_name="c", subcore_axis_name="s", num_cores=1,
                                  num_subcores=N_EXPERTS)

    @functools.partial(
        pl.kernel,
        out_type=(jax.ShapeDtypeStruct((N_EXPERTS * cap,), I32),
                  jax.ShapeDtypeStruct((N_EXPERTS * cap,), F32)),
        mesh=mesh,
        scratch_types=[pltpu.VMEM((n,), F32), pltpu.VMEM((cap,), I32), pltpu.VMEM((cap,), F32),
                       pltpu.VMEM((lanes,), F32), pltpu.VMEM((lanes,), I32)],
        compiler_params=pltpu.CompilerParams(needs_layout_passes=False),
        name="topk_compact",
    )
    def compact(aff_hbm, thr_hbm, need_hbm, idx_hbm, gate_hbm, row, idx_v, gate_v, thr_v, need_v):
        e = lax.axis_index("s")
        pltpu.sync_copy(aff_hbm.at[pl.ds(e * n, n)], row)
        pltpu.sync_copy(thr_hbm.at[pl.ds(e * LANE, lanes)], thr_v)
        pltpu.sync_copy(need_hbm.at[pl.ds(e * LANE, lanes)], need_v)
        thr = thr_v[...]
        need = need_v[...]
        lane = lax.iota(I32, lanes)
        zero_i = jnp.zeros((lanes,), I32)

        def init(i, carry):
            idx_v[pl.ds(i * lanes, lanes)] = zero_i
            gate_v[pl.ds(i * lanes, lanes)] = jnp.zeros((lanes,), F32)
            return carry

        lax.fori_loop(0, cap // lanes, init, 0)

        def body(i, carry):
            off, eqs = carry
            v = row[pl.ds(i * lanes, lanes)]
            m_gt = v > thr
            m_eq = v == thr
            eq_rank = plsc.cumsum(m_eq.astype(I32)) + eqs
            take = m_gt | (m_eq & (eq_rank <= need))
            pos = off + plsc.cumsum(take.astype(I32)) - 1
            take = take & (pos < cap)
            plsc.store_scatter(idx_v, [pos], lane + i * lanes, mask=take)
            plsc.store_scatter(gate_v, [pos], v, mask=take)
            off = off + plsc.all_reduce_population_count(take)
            eqs = eqs + plsc.all_reduce_population_count(m_eq)
            return off, eqs

        lax.fori_loop(0, n // lanes, body, (zero_i, zero_i))
        pltpu.sync_copy(idx_v, idx_hbm.at[pl.ds(e * cap, cap)])
        pltpu.sync_copy(gate_v, gate_hbm.at[pl.ds(e * cap, cap)])

    return compact(aff_flat, thr_flat, need_flat)


def _expert_choice(aff, cap):
    n = aff.shape[1]
    thr, need = _threshold(aff, cap)
    idx, gate = _compact_sc(aff.reshape(-1), thr.reshape(-1), need.reshape(-1), n, cap)
    return gate.reshape(N_EXPERTS, cap), idx.reshape(N_EXPERTS, cap)


def _moe(xy, p, layer):
    n = xy.shape[0]
    cap = EC_CAPACITY_FACTOR * n // N_EXPERTS
    nw = p["norm_ffn"][layer]
    aff = _router(xy, nw, p["w_router_t"][layer])
    gate, idx = _expert_choice(aff, cap)
    return _expert_ffn(xy, idx, gate, nw, p["w_gate"], p["w_up"], p["w_down"], layer)


def _final_norm_kernel(xy_ref, nw_ref, o_ref):
    o_ref[...] = _rms_rows(_from_tok_rows(xy_ref[...]), nw_ref[...])


def _final_norm(xy, nw):
    n = xy.shape[0]
    return pl.pallas_call(
        _final_norm_kernel,
        grid=(n // PROJ_TM,),
        in_specs=[pl.BlockSpec((PROJ_TM, X_ROWS, LANE), lambda i: (i, 1, 0)),
                  pl.BlockSpec((1, D_MODEL), lambda i: (0, 0))],
        out_specs=pl.BlockSpec((PROJ_TM, D_MODEL), lambda i: (i, 0)),
        out_shape=jax.ShapeDtypeStruct((n, D_MODEL), F32),
        compiler_params=_cparams(("arbitrary",)),
        name="final_norm",
    )(xy, nw)


def _rope_tables(seq):
    half = RET_DK // 2
    inv = 1.0 / (ROPE_BASE ** (jnp.arange(half, dtype=F32) / half))
    ang = jnp.arange(seq, dtype=F32)[:, None] * inv[None, :]
    return jnp.cos(ang), jnp.sin(ang)


def _trunk(x, p):
    b, t, d = x.shape
    n = b * t
    cos, sin = _rope_tables(t)

    proj = _ret_inproj(x.reshape(n, d), 0, p["norm_mix"][0], p["ret_w_in"], cos, sin, t)
    xy = _retention(proj.reshape(b, t, -1), x, p["ret_decay"], p["ret_w_out"])
    xy = _moe(xy.reshape(n, TOK_ROWS, LANE), p, 0)

    a, lf = _hgrn_inproj(xy, p["norm_mix"][1], p["hgrn_w_in"], p["hgrn_lb"])
    xy = _hgrn(a.reshape(b, t, -1), lf.reshape(b, t, -1), xy.reshape(b, t, TOK_ROWS, LANE),
               p["hgrn_norm"], p["hgrn_w_out"])
    xy = _moe(xy.reshape(n, TOK_ROWS, LANE), p, 1)

    return _final_norm(xy, p["norm_final"]).reshape(b, t, d)


def kernel(x_prompt, x_sample, norm_mix_w, norm_ffn_w, norm_final_w, ret_w_in, ret_w_out,
           hgrn_w_in, hgrn_lb, hgrn_norm_w, hgrn_w_out, moe_w_router, moe_w_gate, moe_w_up,
           moe_w_down):
    depth = norm_mix_w.shape[0]
    assert depth == 2 and ret_w_in.shape[0] == 1 and hgrn_w_in.shape[0] == 1
    sm = jax.nn.softmax(hgrn_lb.astype(F32), axis=0)
    lower_bounds = jnp.cumsum(sm, axis=0) - sm[0:1]
    p = {
        "norm_mix": norm_mix_w.reshape(depth, 1, D_MODEL),
        "norm_ffn": norm_ffn_w.reshape(depth, 1, D_MODEL),
        "norm_final": norm_final_w.reshape(1, D_MODEL),
        "ret_w_in": ret_w_in[0].astype(BF16),
        "ret_w_out": ret_w_out[0].astype(BF16),
        "ret_decay": _ret_decay_matrix(),
        "hgrn_w_in": hgrn_w_in[0].astype(BF16),
        "hgrn_lb": lower_bounds[1].reshape(1, 2 * D_MODEL),
        "hgrn_norm": hgrn_norm_w[0].reshape(1, D_MODEL),
        "hgrn_w_out": hgrn_w_out[0].astype(BF16),
        "w_router_t": jnp.swapaxes(moe_w_router, 1, 2),
        "w_gate": moe_w_gate.astype(BF16),
        "w_up": moe_w_up.astype(BF16),
        "w_down": moe_w_down.astype(BF16),
    }
    return _trunk(x_prompt, p), _trunk(x_sample, p)
```

```python
import functools
import math

import jax
import jax.numpy as jnp
from jax import lax
from jax.experimental import pallas as pl
from jax.experimental.pallas import tpu as pltpu
from jax.experimental.pallas import tpu_sc as plsc

F32 = jnp.float32
I32 = jnp.int32
BF16 = jnp.bfloat16

D_MODEL = 1024
NORM_EPS = 1e-6
ROPE_BASE = 10000.0

RET_HEADS = 4
RET_DK = 256
RET_DV = 512
RET_QK_WIDTH = RET_HEADS * RET_DK
RET_V_WIDTH = RET_HEADS * RET_DV
RET_DECAY_FWD = 5.0
RET_DECAY_BWD = 5.5
RET_BLOCK = 512

HGRN_HEADS = 8
HGRN_DK = 128
HGRN_CHUNK = 64
HGRN_BLOCK = 256

N_EXPERTS = 16
EC_CAPACITY_FACTOR = 2
EXPERT_FF = 2 * D_MODEL

PROJ_TM = 512
PROJ_CW = 512
FFN_TM = 512
FFN_FW = 512
LANE = 128
SC_LANES = 16
X_ROWS = D_MODEL // LANE
TOK_ROWS = 2 * X_ROWS

VMEM_LIMIT_BYTES = 52 * 1024 * 1024


def _cparams(sem):
    return pltpu.CompilerParams(dimension_semantics=sem, vmem_limit_bytes=VMEM_LIMIT_BYTES)


def _ret_log_gamma(offset):
    return [math.log1p(-(2.0 ** (-offset - h))) for h in range(RET_HEADS)]


def _rms_rows(x, w):
    ms = jnp.mean(x * x, axis=-1, keepdims=True)
    return x * lax.rsqrt(ms + NORM_EPS) * w


def _sigmoid(x):
    return 0.5 * jnp.tanh(0.5 * x) + 0.5


def _silu(x):
    h = 0.5 * x
    return h + h * jnp.tanh(h)


def _from_tok_rows(v):
    return v.reshape(v.shape[0], D_MODEL)


def _to_tok_rows(v):
    return v.reshape(v.shape[0], X_ROWS, LANE)


def _store_xy(xy_ref, val):
    r = _to_tok_rows(val)
    xy_ref[:, 0:X_ROWS, :] = r
    xy_ref[:, X_ROWS:TOK_ROWS, :] = r


def _ret_inproj_kernel(x_ref, nw_ref, w_ref, cos_ref, sin_ref, o_ref, xn_scr):
    xn_scr[...] = _rms_rows(x_ref[...], nw_ref[...]).astype(BF16)
    cos = cos_ref[...]
    sin = sin_ref[...]
    width = w_ref.shape[1]
    half = RET_DK // 2
    for j in range(width // PROJ_CW):
        c0 = j * PROJ_CW
        r = jnp.dot(xn_scr[...], w_ref[:, c0:c0 + PROJ_CW], preferred_element_type=F32)
        if c0 < 2 * RET_QK_WIDTH:
            scale = 1.0 if c0 < RET_QK_WIDTH else RET_DK ** -0.5
            parts = []
            for hh in range(PROJ_CW // RET_DK):
                x1 = r[:, hh * RET_DK:hh * RET_DK + half]
                x2 = r[:, hh * RET_DK + half:(hh + 1) * RET_DK]
                parts.append((x1 * cos - x2 * sin) * scale)
                parts.append((x1 * sin + x2 * cos) * scale)
            r = jnp.concatenate(parts, axis=-1)
        elif c0 >= 2 * RET_QK_WIDTH + RET_V_WIDTH:
            r = _silu(r)
        o_ref[:, c0:c0 + PROJ_CW] = r.astype(BF16)


def _ret_inproj(x2d, col, nw, w, cos, sin, seq):
    n = x2d.shape[0]
    width = w.shape[1]
    tpb = seq // PROJ_TM
    return pl.pallas_call(
        _ret_inproj_kernel,
        grid=(n // PROJ_TM,),
        in_specs=[
            pl.BlockSpec((PROJ_TM, D_MODEL), lambda i: (i, col)),
            pl.BlockSpec((1, D_MODEL), lambda i: (0, 0)),
            pl.BlockSpec((D_MODEL, width), lambda i: (0, 0)),
            pl.BlockSpec((PROJ_TM, RET_DK // 2), lambda i: (i % tpb, 0)),
            pl.BlockSpec((PROJ_TM, RET_DK // 2), lambda i: (i % tpb, 0)),
        ],
        out_specs=pl.BlockSpec((PROJ_TM, width), lambda i: (i, 0)),
        out_shape=jax.ShapeDtypeStruct((n, width), BF16),
        scratch_shapes=[pltpu.VMEM((PROJ_TM, D_MODEL), BF16)],
        compiler_params=_cparams(("arbitrary",)),
        name="ret_inproj",
    )(x2d, nw, w, cos, sin)


def _hgrn_inproj_kernel(xy_ref, nw_ref, w_ref, lb_ref, a_ref, l_ref, xn_scr):
    xn_scr[...] = _rms_rows(_from_tok_rows(xy_ref[...]), nw_ref[...]).astype(BF16)
    width = w_ref.shape[1]
    for j in range(width // PROJ_CW):
        c0 = j * PROJ_CW
        r = jnp.dot(xn_scr[...], w_ref[:, c0:c0 + PROJ_CW], preferred_element_type=F32)
        if c0 < D_MODEL:
            a_ref[:, c0:c0 + PROJ_CW] = _silu(r).astype(BF16)
        elif c0 < 2 * D_MODEL:
            a_ref[:, c0:c0 + PROJ_CW] = r.astype(BF16)
        elif c0 < 4 * D_MODEL:
            g0 = c0 - 2 * D_MODEL
            lb = lb_ref[:, g0:g0 + PROJ_CW]
            f = 0.5 * (1.0 + lb) + (0.5 * (1.0 - lb)) * jnp.tanh(0.5 * r)
            a_ref[:, c0:c0 + PROJ_CW] = (1.0 - f).astype(BF16)
            l_ref[:, g0:g0 + PROJ_CW] = jnp.log(f)
        else:
            a_ref[:, c0:c0 + PROJ_CW] = _sigmoid(r).astype(BF16)


def _hgrn_inproj(xy, nw, w, lb):
    n = xy.shape[0]
    width = w.shape[1]
    return pl.pallas_call(
        _hgrn_inproj_kernel,
        grid=(n // PROJ_TM,),
        in_specs=[
            pl.BlockSpec((PROJ_TM, X_ROWS, LANE), lambda i: (i, 1, 0)),
            pl.BlockSpec((1, D_MODEL), lambda i: (0, 0)),
            pl.BlockSpec((D_MODEL, width), lambda i: (0, 0)),
            pl.BlockSpec((1, 2 * D_MODEL), lambda i: (0, 0)),
        ],
        out_specs=[
            pl.BlockSpec((PROJ_TM, width), lambda i: (i, 0)),
            pl.BlockSpec((PROJ_TM, 2 * D_MODEL), lambda i: (i, 0)),
        ],
        out_shape=[
            jax.ShapeDtypeStruct((n, width), BF16),
            jax.ShapeDtypeStruct((n, 2 * D_MODEL), F32),
        ],
        scratch_shapes=[pltpu.VMEM((PROJ_TM, D_MODEL), BF16)],
        compiler_params=_cparams(("arbitrary",)),
        name="hgrn_inproj",
    )(xy, nw, w, lb)


_NT = (((1,), (1,)), ((), ()))
_TN = (((0,), (0,)), ((), ()))


def _ret_bwd_kernel(q_ref, k_ref, v_ref, ob_ref, r_scr, *, lg):
    @pl.when(pl.program_id(1) == 0)
    def _():
        r_scr[...] = jnp.zeros_like(r_scr)

    c = q_ref.shape[0]
    pos = lax.broadcasted_iota(jnp.int32, (c, 1), 0).astype(F32)
    for h in range(RET_HEADS):
        qh = q_ref[:, h * RET_DK:(h + 1) * RET_DK].astype(F32)
        kh = k_ref[:, h * RET_DK:(h + 1) * RET_DK].astype(F32)
        vh = v_ref[:, h * RET_DV:(h + 1) * RET_DV]
        qd = (qh * jnp.exp((c - pos) * lg[h])).astype(BF16)
        o = jnp.dot(qd, r_scr[h].astype(BF16), preferred_element_type=F32)
        ob_ref[:, h * RET_DV:(h + 1) * RET_DV] = o.astype(BF16)
        kd = (kh * jnp.exp(pos * lg[h])).astype(BF16)
        r_scr[h] = r_scr[h] * math.exp(c * lg[h]) + lax.dot_general(
            kd, vh, _TN, preferred_element_type=F32)


def _ret_fwd_kernel(q_ref, k_ref, v_ref, g_ref, ob_ref, x_ref, m_ref, wo_ref, xy_ref,
                    r_scr, *, lg):
    @pl.when(pl.program_id(1) == 0)
    def _():
        r_scr[...] = jnp.zeros_like(r_scr)

    c = q_ref.shape[0]
    pos = lax.broadcasted_iota(jnp.int32, (c, 1), 0).astype(F32)
    acc = x_ref[...]
    for h in range(RET_HEADS):
        qb = q_ref[:, h * RET_DK:(h + 1) * RET_DK]
        kb = k_ref[:, h * RET_DK:(h + 1) * RET_DK]
        vh = v_ref[:, h * RET_DV:(h + 1) * RET_DV]
        s = lax.dot_general(qb, kb, _NT, preferred_element_type=F32)
        sm = (s * m_ref[h]).astype(BF16)
        qd = (qb.astype(F32) * jnp.exp((pos + 1.0) * lg[h])).astype(BF16)
        o = (jnp.dot(sm, vh, preferred_element_type=F32)
             + jnp.dot(qd, r_scr[h].astype(BF16), preferred_element_type=F32)
             + ob_ref[:, h * RET_DV:(h + 1) * RET_DV].astype(F32))
        kd = (kb.astype(F32) * jnp.exp((c - 1.0 - pos) * lg[h])).astype(BF16)
        r_scr[h] = r_scr[h] * math.exp(c * lg[h]) + lax.dot_general(
            kd, vh, _TN, preferred_element_type=F32)
        ms = jnp.mean(o * o, axis=-1, keepdims=True)
        gated = (o * lax.rsqrt(ms + NORM_EPS)
                 * g_ref[:, h * RET_DV:(h + 1) * RET_DV].astype(F32)).astype(BF16)
        acc = acc + jnp.dot(gated, wo_ref[h * RET_DV:(h + 1) * RET_DV, :],
                            preferred_element_type=F32)
    _store_xy(xy_ref, acc)


def _retention(proj, x3, decay_mat, w_out):
    b, t, _ = proj.shape
    c = RET_BLOCK
    nc = t // c
    lgf = _ret_log_gamma(RET_DECAY_FWD)
    lgb = _ret_log_gamma(RET_DECAY_BWD)
    qk = RET_QK_WIDTH
    vw = RET_V_WIDTH
    state = pltpu.VMEM((RET_HEADS, RET_DK, RET_DV), F32)

    ob = pl.pallas_call(
        functools.partial(_ret_bwd_kernel, lg=lgb),
        grid=(b, nc),
        in_specs=[
            pl.BlockSpec((None, c, qk), lambda bi, i: (bi, nc - 1 - i, 0)),
            pl.BlockSpec((None, c, qk), lambda bi, i: (bi, nc - 1 - i, 1)),
            pl.BlockSpec((None, c, vw), lambda bi, i: (bi, nc - 1 - i, 1)),
        ],
        out_specs=pl.BlockSpec((None, c, vw), lambda bi, i: (bi, nc - 1 - i, 0)),
        out_shape=jax.ShapeDtypeStruct((b, t, vw), BF16),
        scratch_shapes=[state],
        compiler_params=_cparams(("arbitrary", "arbitrary")),
        name="ret_bwd",
    )(proj, proj, proj)

    return pl.pallas_call(
        functools.partial(_ret_fwd_kernel, lg=lgf),
        grid=(b, nc),
        in_specs=[
            pl.BlockSpec((None, c, qk), lambda bi, i: (bi, i, 0)),
            pl.BlockSpec((None, c, qk), lambda bi, i: (bi, i, 1)),
            pl.BlockSpec((None, c, vw), lambda bi, i: (bi, i, 1)),
            pl.BlockSpec((None, c, vw), lambda bi, i: (bi, i, 2)),
            pl.BlockSpec((None, c, vw), lambda bi, i: (bi, i, 0)),
            pl.BlockSpec((None, c, D_MODEL), lambda bi, i: (bi, i, 0)),
            pl.BlockSpec((RET_HEADS, c, c), lambda bi, i: (0, 0, 0)),
            pl.BlockSpec((vw, D_MODEL), lambda bi, i: (0, 0)),
        ],
        out_specs=pl.BlockSpec((None, c, TOK_ROWS, LANE), lambda bi, i: (bi, i, 0, 0)),
        out_shape=jax.ShapeDtypeStruct((b, t, TOK_ROWS, LANE), F32),
        scratch_shapes=[state],
        compiler_params=_cparams(("arbitrary", "arbitrary")),
        name="ret_fwd",
    )(proj, proj, proj, proj, ob, x3, decay_mat, w_out)


def _ret_decay_matrix():
    c = RET_BLOCK
    rel = (jnp.arange(c, dtype=F32)[:, None] - jnp.arange(c, dtype=F32)[None, :])[None]
    lgf = jnp.asarray(_ret_log_gamma(RET_DECAY_FWD), F32)[:, None, None]
    lgb = jnp.asarray(_ret_log_gamma(RET_DECAY_BWD), F32)[:, None, None]
    return jnp.where(rel >= 0, jnp.exp(jnp.maximum(rel, 0.0) * lgf),
                     jnp.exp(jnp.maximum(-rel, 0.0) * lgb))


def _tri_cumsum(tri, x):
    hi = x.astype(BF16)
    lo = (x - hi.astype(F32)).astype(BF16)
    return (jnp.dot(tri, hi, preferred_element_type=F32)
            + jnp.dot(tri, lo, preferred_element_type=F32))


def _hgrn_block(q_ref, k_ref, v_ref, lf_ref, s_scr, reverse):
    tb = HGRN_BLOCK
    c = HGRN_CHUNK
    n_sub = tb // c
    shift = c.bit_length() - 1
    row = lax.broadcasted_iota(jnp.int32, (tb, tb), 0)
    col = lax.broadcasted_iota(jnp.int32, (tb, tb), 1)
    same_chunk = lax.shift_right_logical(row, shift) == lax.shift_right_logical(col, shift)
    keep = same_chunk & ((col >= row) if reverse else (col <= row))
    g = _tri_cumsum(keep.astype(BF16), lf_ref[...])
    g_tot = jnp.concatenate(
        [jnp.broadcast_to(g[i * c:i * c + 1, :] if reverse else g[(i + 1) * c - 1:(i + 1) * c, :],
                          (c, g.shape[1])) for i in range(n_sub)], axis=0)
    qg = q_ref[...] * jnp.exp(g).astype(BF16)
    kb = k_ref[...]
    kg = kb * jnp.exp(-g).astype(BF16)
    kd = kb * jnp.exp(g_tot - g).astype(BF16)
    v = v_ref[...]
    order = list(reversed(range(n_sub))) if reverse else list(range(n_sub))
    e_tot = [jnp.exp(g_tot[i * c:i * c + 1, :]) for i in range(n_sub)]
    outs = []
    for h in range(HGRN_HEADS):
        cs = slice(h * HGRN_DK, (h + 1) * HGRN_DK)
        s = lax.dot_general(qg[:, cs], kg[:, cs], _NT, preferred_element_type=F32)
        p = jnp.where(keep, s, 0.0).astype(BF16)
        o_h = jnp.dot(p, v[:, cs], preferred_element_type=F32)
        st = s_scr[h]
        pieces = [None] * n_sub
        for i in order:
            rows = slice(i * c, (i + 1) * c)
            pieces[i] = o_h[rows, :] + lax.dot_general(
                qg[rows, cs], st.astype(BF16), _NT, preferred_element_type=F32)
            st = st * e_tot[i][:, cs] + lax.dot_general(
                v[rows, cs], kd[rows, cs], _TN, preferred_element_type=F32)
        s_scr[h] = st
        outs.append(jnp.concatenate(pieces, axis=0))
    return jnp.concatenate(outs, axis=-1)


def _hgrn_bwd_kernel(q_ref, v_ref, k_ref, lf_ref, ob_ref, s_scr):
    @pl.when(pl.program_id(1) == 0)
    def _():
        s_scr[...] = jnp.zeros_like(s_scr)

    ob_ref[...] = _hgrn_block(q_ref, k_ref, v_ref, lf_ref, s_scr, reverse=True).astype(BF16)


def _hgrn_fwd_kernel(q_ref, v_ref, k_ref, lf_ref, sg_ref, ob_ref, x_ref, nw_ref, wo_ref,
                     xy_ref, s_scr, o_scr):
    @pl.when(pl.program_id(1) == 0)
    def _():
        s_scr[...] = jnp.zeros_like(s_scr)

    o_scr[...] = (_hgrn_block(q_ref, k_ref, v_ref, lf_ref, s_scr, reverse=False)
                  + ob_ref[...].astype(F32))
    on = _rms_rows(o_scr[...], nw_ref[...])
    gated = (on * sg_ref[...].astype(F32)).astype(BF16)
    xn = (_from_tok_rows(x_ref[...])
          + jnp.dot(gated, wo_ref[...], preferred_element_type=F32))
    _store_xy(xy_ref, xn)


def _hgrn(a3, l3, xy3, norm_w, w_out):
    b, t, _ = a3.shape
    tb = HGRN_BLOCK
    nb = t // tb
    d = D_MODEL
    state = pltpu.VMEM((HGRN_HEADS, HGRN_DK, HGRN_DK), F32)

    ob = pl.pallas_call(
        _hgrn_bwd_kernel,
        grid=(b, nb),
        in_specs=[
            pl.BlockSpec((None, tb, d), lambda bi, i: (bi, nb - 1 - i, 0)),
            pl.BlockSpec((None, tb, d), lambda bi, i: (bi, nb - 1 - i, 1)),
            pl.BlockSpec((None, tb, d), lambda bi, i: (bi, nb - 1 - i, 3)),
            pl.BlockSpec((None, tb, d), lambda bi, i: (bi, nb - 1 - i, 1)),
        ],
        out_specs=pl.BlockSpec((None, tb, d), lambda bi, i: (bi, nb - 1 - i, 0)),
        out_shape=jax.ShapeDtypeStruct((b, t, d), BF16),
        scratch_shapes=[state],
        compiler_params=_cparams(("arbitrary", "arbitrary")),
        name="hgrn_bwd",
    )(a3, a3, a3, l3)

    return pl.pallas_call(
        _hgrn_fwd_kernel,
        grid=(b, nb),
        in_specs=[
            pl.BlockSpec((None, tb, d), lambda bi, i: (bi, i, 0)),
            pl.BlockSpec((None, tb, d), lambda bi, i: (bi, i, 1)),
            pl.BlockSpec((None, tb, d), lambda bi, i: (bi, i, 2)),
            pl.BlockSpec((None, tb, d), lambda bi, i: (bi, i, 0)),
            pl.BlockSpec((None, tb, d), lambda bi, i: (bi, i, 4)),
            pl.BlockSpec((None, tb, d), lambda bi, i: (bi, i, 0)),
            pl.BlockSpec((None, tb, X_ROWS, LANE), lambda bi, i: (bi, i, 1, 0)),
            pl.BlockSpec((1, d), lambda bi, i: (0, 0)),
            pl.BlockSpec((d, d), lambda bi, i: (0, 0)),
        ],
        out_specs=pl.BlockSpec((None, tb, TOK_ROWS, LANE), lambda bi, i: (bi, i, 0, 0)),
        out_shape=jax.ShapeDtypeStruct((b, t, TOK_ROWS, LANE), F32),
        scratch_shapes=[state, pltpu.VMEM((tb, d), F32)],
        compiler_params=_cparams(("arbitrary", "arbitrary")),
        name="hgrn_fwd",
    )(a3, a3, a3, l3, a3, ob, xy3, norm_w, w_out)


def _router_kernel(xy_ref, nw_ref, wr_ref, aff_ref):
    xn = _rms_rows(_from_tok_rows(xy_ref[...]), nw_ref[...])
    xh = xn.astype(BF16)
    xl = (xn - xh.astype(F32)).astype(BF16)
    wr = wr_ref[...]
    wh = wr.astype(BF16)
    wl = (wr - wh.astype(F32)).astype(BF16)
    logits = (lax.dot_general(wh, xh, _NT, preferred_element_type=F32)
              + lax.dot_general(wh, xl, _NT, preferred_element_type=F32)
              + lax.dot_general(wl, xh, _NT, preferred_element_type=F32))
    m = jnp.max(logits, axis=0, keepdims=True)
    e = jnp.exp(logits - m)
    aff_ref[...] = e / jnp.sum(e, axis=0, keepdims=True)


def _router(xy, nw, wr_t):
    n = xy.shape[0]
    return pl.pallas_call(
        _router_kernel,
        grid=(n // PROJ_TM,),
        in_specs=[
            pl.BlockSpec((PROJ_TM, X_ROWS, LANE), lambda i: (i, 0, 0)),
            pl.BlockSpec((1, D_MODEL), lambda i: (0, 0)),
            pl.BlockSpec((N_EXPERTS, D_MODEL), lambda i: (0, 0)),
        ],
        out_specs=pl.BlockSpec((N_EXPERTS, PROJ_TM), lambda i: (0, i)),
        out_shape=jax.ShapeDtypeStruct((N_EXPERTS, n), F32),
        compiler_params=_cparams(("arbitrary",)),
        name="router",
    )(xy, nw, wr_t)


def _ffn_kernel(idx_ref, idxf_ref, idxp_ref, idxn_ref, gate_ref, nw_ref, wg_ref, wu_ref, wd_ref,
                xy_in_ref, xy_ref, buf, stage, acc_scr, xn_scr, sem_g, sem_s, *,
                tiles_per_expert):
    del xy_in_ref
    tm = FFN_TM
    e = pl.program_id(0)
    i = pl.program_id(1)
    step = e * tiles_per_expert + i
    n_steps = pl.num_programs(0) * tiles_per_expert
    last_tile = i == tiles_per_expert - 1
    y_rows = pl.ds(X_ROWS, X_ROWS)

    def start_gather(ids, dst_slot, lo=0, hi=FFN_TM):
        for r in range(lo, hi):
            pltpu.make_async_copy(xy_ref.at[ids[0, r]], buf.at[dst_slot * tm + r],
                                  sem_g.at[dst_slot]).start()

    def wait_gather(dst_slot):
        pltpu.make_async_copy(xy_ref.at[pl.ds(0, tm)], buf.at[pl.ds(0, tm)],
                              sem_g.at[dst_slot]).wait()

    def start_scatter(ids, src_slot, lo=0, hi=FFN_TM):
        for r in range(lo, hi):
            pltpu.make_async_copy(stage.at[src_slot * tm + r], xy_ref.at[ids[0, r], y_rows],
                                  sem_s.at[0]).start()

    def wait_scatter():
        pltpu.make_async_copy(stage.at[pl.ds(0, tm)], xy_ref.at[pl.ds(0, tm), y_rows],
                              sem_s.at[0]).wait()

    @pl.when(step == 0)
    def _():
        start_gather(idx_ref, 0)

    def run(slot):
        cur = pl.ds(slot * tm, tm)
        prev = pl.ds((1 - slot) * tm, tm)
        wait_gather(slot)

        @pl.when(i == 0)
        def _():
            stage[prev] = buf[cur, X_ROWS:TOK_ROWS, :]

        xn_scr[...] = _rms_rows(_from_tok_rows(buf[cur, 0:X_ROWS, :]), nw_ref[...]).astype(BF16)

        n_f = EXPERT_FF // FFN_FW
        n_pts = 3 * n_f
        cuts = [(tm * k) // n_pts for k in range(n_pts + 1)]

        def issue(k):
            start_scatter(idxp_ref, 1 - slot, cuts[k], cuts[k + 1])
            start_gather(idxf_ref, 1 - slot, cuts[k], cuts[k + 1])

        eye = (lax.broadcasted_iota(jnp.int32, (LANE, LANE), 0)
               == lax.broadcasted_iota(jnp.int32, (LANE, LANE), 1)).astype(F32)
        gate_col = jnp.concatenate(
            [jnp.sum(eye * gate_ref[r:r + 1, :], axis=1, keepdims=True)
             for r in range(tm // LANE)], axis=0)
        for f in range(n_f):
            fs = slice(f * FFN_FW, (f + 1) * FFN_FW)
            issue(3 * f)
            g = jnp.dot(xn_scr[...], wg_ref[:, fs], preferred_element_type=F32)
            issue(3 * f + 1)
            u = jnp.dot(xn_scr[...], wu_ref[:, fs], preferred_element_type=F32)
            hcur = (g * _sigmoid(g) * u * gate_col).astype(BF16)
            issue(3 * f + 2)
            part = jnp.dot(hcur, wd_ref[fs, :], preferred_element_type=F32)
            if f == 0:
                acc_scr[...] = part
            elif f == 1:
                acc_scr[...] += part + _from_tok_rows(buf[cur, X_ROWS:TOK_ROWS, :])
            elif f < n_f - 1:
                acc_scr[...] += part
            else:
                wait_scatter()
                stage[cur] = _to_tok_rows(acc_scr[...] + part)

        @pl.when(last_tile)
        def _():
            wait_gather(1 - slot)
            start_scatter(idx_ref, slot)
            wait_scatter()

            @pl.when(step < n_steps - 1)
            def _():
                start_gather(idxn_ref, 1 - slot)

    for slot in (0, 1):
        pl.when(step % 2 == slot)(functools.partial(run, slot))


def _expert_ffn(xy, idx, gate, nw, wg, wu, wd, layer):
    cap = idx.shape[1]
    tm = FFN_TM
    nt = cap // tm
    assert nt >= 2 and EXPERT_FF // FFN_FW >= 3
    n_steps = N_EXPERTS * nt
    idx3 = idx.reshape(n_steps, 1, tm)
    gate3 = gate.reshape(n_steps, tm // LANE, LANE)
    smem_tile = functools.partial(pl.BlockSpec, (None, 1, tm), memory_space=pltpu.SMEM)
    return pl.pallas_call(
        functools.partial(_ffn_kernel, tiles_per_expert=nt),
        grid=(N_EXPERTS, nt),
        in_specs=[
            smem_tile(lambda e, i: (e * nt + i, 0, 0)),
            smem_tile(lambda e, i: (e * nt + jnp.minimum(i + 1, nt - 1), 0, 0)),
            smem_tile(lambda e, i: (e * nt + jnp.maximum(i - 1, 0), 0, 0)),
            smem_tile(lambda e, i: (jnp.minimum(e * nt + i + 1, n_steps - 1), 0, 0)),
            pl.BlockSpec((None, tm // LANE, LANE), lambda e, i: (e * nt + i, 0, 0)),
            pl.BlockSpec((1, D_MODEL), lambda e, i: (0, 0)),
            pl.BlockSpec((None, None, D_MODEL, EXPERT_FF), lambda e, i: (layer, e, 0, 0)),
            pl.BlockSpec((None, None, D_MODEL, EXPERT_FF), lambda e, i: (layer, e, 0, 0)),
            pl.BlockSpec((None, None, EXPERT_FF, D_MODEL), lambda e, i: (layer, e, 0, 0)),
            pl.BlockSpec(memory_space=pl.ANY),
        ],
        out_specs=pl.BlockSpec(memory_space=pl.ANY),
        out_shape=jax.ShapeDtypeStruct(xy.shape, F32),
        scratch_shapes=[
            pltpu.VMEM((2 * tm, TOK_ROWS, LANE), F32),
            pltpu.VMEM((2 * tm, X_ROWS, LANE), F32),
            pltpu.VMEM((tm, D_MODEL), F32),
            pltpu.VMEM((tm, D_MODEL), BF16),
            pltpu.SemaphoreType.DMA((2,)),
            pltpu.SemaphoreType.DMA((1,)),
        ],
        input_output_aliases={9: 0},
        compiler_params=_cparams(("arbitrary", "arbitrary")),
        name="expert_ffn",
    )(idx3, idx3, idx3, idx3, gate3, nw, wg, wu, wd, xy)


def _threshold_kernel(aff_ref, thr_ref, need_ref, *, k):
    bits = lax.bitcast_convert_type(aff_ref[...], I32)

    def body(b, t):
        cand = t | (jnp.int32(1) << (30 - b))
        cnt = jnp.sum((bits >= cand).astype(I32), axis=1, keepdims=True)
        return jnp.where(cnt >= k, cand, t)

    t = lax.fori_loop(0, 31, body, jnp.zeros((N_EXPERTS, 1), I32))
    n_gt = jnp.sum((bits > t).astype(I32), axis=1, keepdims=True)
    thr_ref[...] = jnp.broadcast_to(lax.bitcast_convert_type(t, F32), thr_ref.shape)
    need_ref[...] = jnp.broadcast_to(k - n_gt, need_ref.shape)


def _threshold(aff, k):
    return pl.pallas_call(
        functools.partial(_threshold_kernel, k=k),
        out_shape=[jax.ShapeDtypeStruct((N_EXPERTS, LANE), F32),
                   jax.ShapeDtypeStruct((N_EXPERTS, LANE), I32)],
        compiler_params=pltpu.CompilerParams(vmem_limit_bytes=VMEM_LIMIT_BYTES),
        name="topk_threshold",
    )(aff)


def _compact_sc(aff_flat, thr_flat, need_flat, n, cap):
    lanes = SC_LANES
    mesh = plsc.VectorSubcoreMesh(core_axis_name="c", subcore_axis_name="s", num_cores=1,
                                  num_subcores=N_EXPERTS)

    @functools.partial(
        pl.kernel,
        out_type=(jax.ShapeDtypeStruct((N_EXPERTS * cap,), I32),
                  jax.ShapeDtypeStruct((N_EXPERTS * cap,), F32)),
        mesh=mesh,
        scratch_types=[pltpu.VMEM((n,), F32), pltpu.VMEM((cap,), I32), pltpu.VMEM((cap,), F32),
                       pltpu.VMEM((lanes,), F32), pltpu.VMEM((lanes,), I32)],
        compiler_params=pltpu.CompilerParams(needs_layout_passes=False),
        name="topk_compact",
    )
    def compact(aff_hbm, thr_hbm, need_hbm, idx_hbm, gate_hbm, row, idx_v, gate_v, thr_v, need_v):
        e = lax.axis_index("s")
        pltpu.sync_copy(aff_hbm.at[pl.ds(e * n, n)], row)
        pltpu.sync_copy(thr_hbm.at[pl.ds(e * LANE, lanes)], thr_v)
        pltpu.sync_copy(need_hbm.at[pl.ds(e * LANE, lanes)], need_v)
        thr = thr_v[...]
        need = need_v[...]
        lane = lax.iota(I32, lanes)
        zero_i = jnp.zeros((lanes,), I32)

        def init(i, carry):
            idx_v[pl.ds(i * lanes, lanes)] = zero_i
            gate_v[pl.ds(i * lanes, lanes)] = jnp.zeros((lanes,), F32)
            return carry

        lax.fori_loop(0, cap // lanes, init, 0)

        def body(i, carry):
            off, eqs = carry
            v = row[pl.ds(i * lanes, lanes)]
            m_gt = v > thr
            m_eq = v == thr
            eq_rank = plsc.cumsum(m_eq.astype(I32)) + eqs
            take = m_gt | (m_eq & (eq_rank <= need))
            pos = off + plsc.cumsum(take.astype(I32)) - 1
            take = take & (pos < cap)
            plsc.store_scatter(idx_v, [pos], lane + i * lanes, mask=take)
            plsc.store_scatter(gate_v, [pos], v, mask=take)
            off = off + plsc.all_reduce_population_count(take)
            eqs = eqs + plsc.all_reduce_population_count(m_eq)
            return off, eqs

        lax.fori_loop(0, n // lanes, body, (zero_i, zero_i))
        pltpu.sync_copy(idx_v, idx_hbm.at[pl.ds(e * cap, cap)])
        pltpu.sync_copy(gate_v, gate_hbm.at[pl.ds(e * cap, cap)])

    return compact(aff_flat, thr_flat, need_flat)


def _expert_choice(aff, cap):
    n = aff.shape[1]
    thr, need = _threshold(aff, cap)
    idx, gate = _compact_sc(aff.reshape(-1), thr.reshape(-1), need.reshape(-1), n, cap)
    return gate.reshape(N_EXPERTS, cap), idx.reshape(N_EXPERTS, cap)


def _moe(xy, p, layer):
    n = xy.shape[0]
    cap = EC_CAPACITY_FACTOR * n // N_EXPERTS
    nw = p["norm_ffn"][layer]
    aff = _router(xy, nw, p["w_router_t"][layer])
    gate, idx = _expert_choice(aff, cap)
    return _expert_ffn(xy, idx, gate, nw, p["w_gate"], p["w_up"], p["w_down"], layer)


def _final_norm_kernel(xy_ref, nw_ref, o_ref):
    o_ref[...] = _rms_rows(_from_tok_rows(xy_ref[...]), nw_ref[...])


def _final_norm(xy, nw):
    n = xy.shape[0]
    return pl.pallas_call(
        _final_norm_kernel,
        grid=(n // PROJ_TM,),
        in_specs=[pl.BlockSpec((PROJ_TM, X_ROWS, LANE), lambda i: (i, 1, 0)),
                  pl.BlockSpec((1, D_MODEL), lambda i: (0, 0))],
        out_specs=pl.BlockSpec((PROJ_TM, D_MODEL), lambda i: (i, 0)),
        out_shape=jax.ShapeDtypeStruct((n, D_MODEL), F32),
        compiler_params=_cparams(("arbitrary",)),
        name="final_norm",
    )(xy, nw)


def _rope_tables(seq):
    half = RET_DK // 2
    inv = 1.0 / (ROPE_BASE ** (jnp.arange(half, dtype=F32) / half))
    ang = jnp.arange(seq, dtype=F32)[:, None] * inv[None, :]
    return jnp.cos(ang), jnp.sin(ang)


def _trunk(x, p):
    b, t, d = x.shape
    n = b * t
    cos, sin = _rope_tables(t)

    proj = _ret_inproj(x.reshape(n, d), 0, p["norm_mix"][0], p["ret_w_in"], cos, sin, t)
    xy = _retention(proj.reshape(b, t, -1), x, p["ret_decay"], p["ret_w_out"])
    xy = _moe(xy.reshape(n, TOK_ROWS, LANE), p, 0)

    a, lf = _hgrn_inproj(xy, p["norm_mix"][1], p["hgrn_w_in"], p["hgrn_lb"])
    xy = _hgrn(a.reshape(b, t, -1), lf.reshape(b, t, -1), xy.reshape(b, t, TOK_ROWS, LANE),
               p["hgrn_norm"], p["hgrn_w_out"])
    xy = _moe(xy.reshape(n, TOK_ROWS, LANE), p, 1)

    return _final_norm(xy, p["norm_final"]).reshape(b, t, d)


def kernel(x_prompt, x_sample, norm_mix_w, norm_ffn_w, norm_final_w, ret_w_in, ret_w_out,
           hgrn_w_in, hgrn_lb, hgrn_norm_w, hgrn_w_out, moe_w_router, moe_w_gate, moe_w_up,
           moe_w_down):
    depth = norm_mix_w.shape[0]
    assert depth == 2 and ret_w_in.shape[0] == 1 and hgrn_w_in.shape[0] == 1
    sm = jax.nn.softmax(hgrn_lb.astype(F32), axis=0)
    lower_bounds = jnp.cumsum(sm, axis=0) - sm[0:1]
    p = {
        "norm_mix": norm_mix_w.reshape(depth, 1, D_MODEL),
        "norm_ffn": norm_ffn_w.reshape(depth, 1, D_MODEL),
        "norm_final": norm_final_w.reshape(1, D_MODEL),
        "ret_w_in": ret_w_in[0].astype(BF16),
        "ret_w_out": ret_w_out[0].astype(BF16),
        "ret_decay": _ret_decay_matrix(),
        "hgrn_w_in": hgrn_w_in[0].astype(BF16),
        "hgrn_lb": lower_bounds[1].reshape(1, 2 * D_MODEL),
        "hgrn_norm": hgrn_norm_w[0].reshape(1, D_MODEL),
        "hgrn_w_out": hgrn_w_out[0].astype(BF16),
        "w_router_t": jnp.swapaxes(moe_w_router, 1, 2),
        "w_gate": moe_w_gate.astype(BF16),
        "w_up": moe_w_up.astype(BF16),
        "w_down": moe_w_down.astype(BF16),
    }
    return _trunk(x_prompt, p), _trunk(x_sample, p)
```

```python
import functools
import math

import jax
import jax.numpy as jnp
from jax import lax
from jax.experimental import pallas as pl
from jax.experimental.pallas import tpu as pltpu
from jax.experimental.pallas import tpu_sc as plsc

F32 = jnp.float32
I32 = jnp.int32
BF16 = jnp.bfloat16

D_MODEL = 1024
NORM_EPS = 1e-6
ROPE_BASE = 10000.0

RET_HEADS = 4
RET_DK = 256
RET_DV = 512
RET_QK_WIDTH = RET_HEADS * RET_DK
RET_V_WIDTH = RET_HEADS * RET_DV
RET_DECAY_FWD = 5.0
RET_DECAY_BWD = 5.5
RET_BLOCK = 512

HGRN_HEADS = 8
HGRN_DK = 128
HGRN_CHUNK = 64
HGRN_BLOCK = 256

N_EXPERTS = 16
EC_CAPACITY_FACTOR = 2
EXPERT_FF = 2 * D_MODEL

PROJ_TM = 512
PROJ_CW = 512
FFN_TM = 512
FFN_FW = 512
LANE = 128
SC_LANES = 16
X_ROWS = D_MODEL // LANE
TOK_ROWS = 2 * X_ROWS

VMEM_LIMIT_BYTES = 52 * 1024 * 1024


def _cparams(sem):
    return pltpu.CompilerParams(dimension_semantics=sem, vmem_limit_bytes=VMEM_LIMIT_BYTES)


def _ret_log_gamma(offset):
    return [math.log1p(-(2.0 ** (-offset - h))) for h in range(RET_HEADS)]


def _rms_rows(x, w):
    ms = jnp.mean(x * x, axis=-1, keepdims=True)
    return x * lax.rsqrt(ms + NORM_EPS) * w


def _sigmoid(x):
    return 0.5 * jnp.tanh(0.5 * x) + 0.5


def _silu(x):
    h = 0.5 * x
    return h + h * jnp.tanh(h)


def _from_tok_rows(v):
    return v.reshape(v.shape[0], D_MODEL)


def _to_tok_rows(v):
    return v.reshape(v.shape[0], X_ROWS, LANE)


def _store_xy(xy_ref, val):
    r = _to_tok_rows(val)
    xy_ref[:, 0:X_ROWS, :] = r
    xy_ref[:, X_ROWS:TOK_ROWS, :] = r


def _ret_inproj_kernel(x_ref, nw_ref, w_ref, cos_ref, sin_ref, o_ref, xn_scr):
    xn_scr[...] = _rms_rows(x_ref[...], nw_ref[...]).astype(BF16)
    cos = cos_ref[...]
    sin = sin_ref[...]
    width = w_ref.shape[1]
    half = RET_DK // 2
    for j in range(width // PROJ_CW):
        c0 = j * PROJ_CW
        r = jnp.dot(xn_scr[...], w_ref[:, c0:c0 + PROJ_CW], preferred_element_type=F32)
        if c0 < 2 * RET_QK_WIDTH:
            scale = 1.0 if c0 < RET_QK_WIDTH else RET_DK ** -0.5
            parts = []
            for hh in range(PROJ_CW // RET_DK):
                x1 = r[:, hh * RET_DK:hh * RET_DK + half]
                x2 = r[:, hh * RET_DK + half:(hh + 1) * RET_DK]
                parts.append((x1 * cos - x2 * sin) * scale)
                parts.append((x1 * sin + x2 * cos) * scale)
            r = jnp.concatenate(parts, axis=-1)
        elif c0 >= 2 * RET_QK_WIDTH + RET_V_WIDTH:
            r = _silu(r)
        o_ref[:, c0:c0 + PROJ_CW] = r.astype(BF16)


def _ret_inproj(x2d, col, nw, w, cos, sin, seq):
    n = x2d.shape[0]
    width = w.shape[1]
    tpb = seq // PROJ_TM
    return pl.pallas_call(
        _ret_inproj_kernel,
        grid=(n // PROJ_TM,),
        in_specs=[
            pl.BlockSpec((PROJ_TM, D_MODEL), lambda i: (i, col)),
            pl.BlockSpec((1, D_MODEL), lambda i: (0, 0)),
            pl.BlockSpec((D_MODEL, width), lambda i: (0, 0)),
            pl.BlockSpec((PROJ_TM, RET_DK // 2), lambda i: (i % tpb, 0)),
            pl.BlockSpec((PROJ_TM, RET_DK // 2), lambda i: (i % tpb, 0)),
        ],
        out_specs=pl.BlockSpec((PROJ_TM, width), lambda i: (i, 0)),
        out_shape=jax.ShapeDtypeStruct((n, width), BF16),
        scratch_shapes=[pltpu.VMEM((PROJ_TM, D_MODEL), BF16)],
        compiler_params=_cparams(("arbitrary",)),
        name="ret_inproj",
    )(x2d, nw, w, cos, sin)


def _hgrn_inproj_kernel(xy_ref, nw_ref, w_ref, lb_ref, a_ref, l_ref, xn_scr):
    xn_scr[...] = _rms_rows(_from_tok_rows(xy_ref[...]), nw_ref[...]).astype(BF16)
    width = w_ref.shape[1]
    for j in range(width // PROJ_CW):
        c0 = j * PROJ_CW
        r = jnp.dot(xn_scr[...], w_ref[:, c0:c0 + PROJ_CW], preferred_element_type=F32)
        if c0 < D_MODEL:
            a_ref[:, c0:c0 + PROJ_CW] = _silu(r).astype(BF16)
        elif c0 < 2 * D_MODEL:
            a_ref[:, c0:c0 + PROJ_CW] = r.astype(BF16)
        elif c0 < 4 * D_MODEL:
            g0 = c0 - 2 * D_MODEL
            lb = lb_ref[:, g0:g0 + PROJ_CW]
            f = 0.5 * (1.0 + lb) + (0.5 * (1.0 - lb)) * jnp.tanh(0.5 * r)
            a_ref[:, c0:c0 + PROJ_CW] = (1.0 - f).astype(BF16)
            l_ref[:, g0:g0 + PROJ_CW] = jnp.log(f)
        else:
            a_ref[:, c0:c0 + PROJ_CW] = _sigmoid(r).astype(BF16)


def _hgrn_inproj(xy, nw, w, lb):
    n = xy.shape[0]
    width = w.shape[1]
    return pl.pallas_call(
        _hgrn_inproj_kernel,
        grid=(n // PROJ_TM,),
        in_specs=[
            pl.BlockSpec((PROJ_TM, X_ROWS, LANE), lambda i: (i, 1, 0)),
            pl.BlockSpec((1, D_MODEL), lambda i: (0, 0)),
            pl.BlockSpec((D_MODEL, width), lambda i: (0, 0)),
            pl.BlockSpec((1, 2 * D_MODEL), lambda i: (0, 0)),
        ],
        out_specs=[
            pl.BlockSpec((PROJ_TM, width), lambda i: (i, 0)),
            pl.BlockSpec((PROJ_TM, 2 * D_MODEL), lambda i: (i, 0)),
        ],
        out_shape=[
            jax.ShapeDtypeStruct((n, width), BF16),
            jax.ShapeDtypeStruct((n, 2 * D_MODEL), F32),
        ],
        scratch_shapes=[pltpu.VMEM((PROJ_TM, D_MODEL), BF16)],
        compiler_params=_cparams(("arbitrary",)),
        name="hgrn_inproj",
    )(xy, nw, w, lb)


_NT = (((1,), (1,)), ((), ()))
_TN = (((0,), (0,)), ((), ()))


def _ret_bwd_kernel(q_ref, k_ref, v_ref, ob_ref, r_scr, *, lg):
    @pl.when(pl.program_id(1) == 0)
    def _():
        r_scr[...] = jnp.zeros_like(r_scr)

    c = q_ref.shape[0]
    pos = lax.broadcasted_iota(jnp.int32, (c, 1), 0).astype(F32)
    for h in range(RET_HEADS):
        qh = q_ref[:, h * RET_DK:(h + 1) * RET_DK].astype(F32)
        kh = k_ref[:, h * RET_DK:(h + 1) * RET_DK].astype(F32)
        vh = v_ref[:, h * RET_DV:(h + 1) * RET_DV]
        qd = (qh * jnp.exp((c - pos) * lg[h])).astype(BF16)
        o = jnp.dot(qd, r_scr[h].astype(BF16), preferred_element_type=F32)
        ob_ref[:, h * RET_DV:(h + 1) * RET_DV] = o.astype(BF16)
        kd = (kh * jnp.exp(pos * lg[h])).astype(BF16)
        r_scr[h] = r_scr[h] * math.exp(c * lg[h]) + lax.dot_general(
            kd, vh, _TN, preferred_element_type=F32)


def _ret_fwd_kernel(q_ref, k_ref, v_ref, g_ref, ob_ref, x_ref, m_ref, wo_ref, xy_ref,
                    r_scr, *, lg):
    @pl.when(pl.program_id(1) == 0)
    def _():
        r_scr[...] = jnp.zeros_like(r_scr)

    c = q_ref.shape[0]
    pos = lax.broadcasted_iota(jnp.int32, (c, 1), 0).astype(F32)
    acc = x_ref[...]
    for h in range(RET_HEADS):
        qb = q_ref[:, h * RET_DK:(h + 1) * RET_DK]
        kb = k_ref[:, h * RET_DK:(h + 1) * RET_DK]
        vh = v_ref[:, h * RET_DV:(h + 1) * RET_DV]
        s = lax.dot_general(qb, kb, _NT, preferred_element_type=F32)
        sm = (s * m_ref[h]).astype(BF16)
        qd = (qb.astype(F32) * jnp.exp((pos + 1.0) * lg[h])).astype(BF16)
        o = (jnp.dot(sm, vh, preferred_element_type=F32)
             + jnp.dot(qd, r_scr[h].astype(BF16), preferred_element_type=F32)
             + ob_ref[:, h * RET_DV:(h + 1) * RET_DV].astype(F32))
        kd = (kb.astype(F32) * jnp.exp((c - 1.0 - pos) * lg[h])).astype(BF16)
        r_scr[h] = r_scr[h] * math.exp(c * lg[h]) + lax.dot_general(
            kd, vh, _TN, preferred_element_type=F32)
        ms = jnp.mean(o * o, axis=-1, keepdims=True)
        gated = (o * lax.rsqrt(ms + NORM_EPS)
                 * g_ref[:, h * RET_DV:(h + 1) * RET_DV].astype(F32)).astype(BF16)
        acc = acc + jnp.dot(gated, wo_ref[h * RET_DV:(h + 1) * RET_DV, :],
                            preferred_element_type=F32)
    _store_xy(xy_ref, acc)


def _retention(proj, x3, decay_mat, w_out):
    b, t, _ = proj.shape
    c = RET_BLOCK
    nc = t // c
    lgf = _ret_log_gamma(RET_DECAY_FWD)
    lgb = _ret_log_gamma(RET_DECAY_BWD)
    qk = RET_QK_WIDTH
    vw = RET_V_WIDTH
    state = pltpu.VMEM((RET_HEADS, RET_DK, RET_DV), F32)

    ob = pl.pallas_call(
        functools.partial(_ret_bwd_kernel, lg=lgb),
        grid=(b, nc),
        in_specs=[
            pl.BlockSpec((None, c, qk), lambda bi, i: (bi, nc - 1 - i, 0)),
            pl.BlockSpec((None, c, qk), lambda bi, i: (bi, nc - 1 - i, 1)),
            pl.BlockSpec((None, c, vw), lambda bi, i: (bi, nc - 1 - i, 1)),
        ],
        out_specs=pl.BlockSpec((None, c, vw), lambda bi, i: (bi, nc - 1 - i, 0)),
        out_shape=jax.ShapeDtypeStruct((b, t, vw), BF16),
        scratch_shapes=[state],
        compiler_params=_cparams(("arbitrary", "arbitrary")),
        name="ret_bwd",
    )(proj, proj, proj)

    return pl.pallas_call(
        functools.partial(_ret_fwd_kernel, lg=lgf),
        grid=(b, nc),
        in_specs=[
            pl.BlockSpec((None, c, qk), lambda bi, i: (bi, i, 0)),
            pl.BlockSpec((None, c, qk), lambda bi, i: (bi, i, 1)),
            pl.BlockSpec((None, c, vw), lambda bi, i: (bi, i, 1)),
            pl.BlockSpec((None, c, vw), lambda bi, i: (bi, i, 2)),
            pl.BlockSpec((None, c, vw), lambda bi, i: (bi, i, 0)),
            pl.BlockSpec((None, c, D_MODEL), lambda bi, i: (bi, i, 0)),
            pl.BlockSpec((RET_HEADS, c, c), lambda bi, i: (0, 0, 0)),
            pl.BlockSpec((vw, D_MODEL), lambda bi, i: (0, 0)),
        ],
        out_specs=pl.BlockSpec((None, c, TOK_ROWS, LANE), lambda bi, i: (bi, i, 0, 0)),
        out_shape=jax.ShapeDtypeStruct((b, t, TOK_ROWS, LANE), F32),
        scratch_shapes=[state],
        compiler_params=_cparams(("arbitrary", "arbitrary")),
        name="ret_fwd",
    )(proj, proj, proj, proj, ob, x3, decay_mat, w_out)


def _ret_decay_matrix():
    c = RET_BLOCK
    rel = (jnp.arange(c, dtype=F32)[:, None] - jnp.arange(c, dtype=F32)[None, :])[None]
    lgf = jnp.asarray(_ret_log_gamma(RET_DECAY_FWD), F32)[:, None, None]
    lgb = jnp.asarray(_ret_log_gamma(RET_DECAY_BWD), F32)[:, None, None]
    return jnp.where(rel >= 0, jnp.exp(jnp.maximum(rel, 0.0) * lgf),
                     jnp.exp(jnp.maximum(-rel, 0.0) * lgb))


def _tri_cumsum(tri, x):
    hi = x.astype(BF16)
    lo = (x - hi.astype(F32)).astype(BF16)
    return (jnp.dot(tri, hi, preferred_element_type=F32)
            + jnp.dot(tri, lo, preferred_element_type=F32))


def _hgrn_block(q_ref, k_ref, v_ref, lf_ref, s_scr, reverse):
    tb = HGRN_BLOCK
    c = HGRN_CHUNK
    n_sub = tb // c
    shift = c.bit_length() - 1
    row = lax.broadcasted_iota(jnp.int32, (tb, tb), 0)
    col = lax.broadcasted_iota(jnp.int32, (tb, tb), 1)
    same_chunk = lax.shift_right_logical(row, shift) == lax.shift_right_logical(col, shift)
    keep = same_chunk & ((col >= row) if reverse else (col <= row))
    g = _tri_cumsum(keep.astype(BF16), lf_ref[...])
    g_tot = jnp.concatenate(
        [jnp.broadcast_to(g[i * c:i * c + 1, :] if reverse else g[(i + 1) * c - 1:(i + 1) * c, :],
                          (c, g.shape[1])) for i in range(n_sub)], axis=0)
    qg = q_ref[...] * jnp.exp(g).astype(BF16)
    kb = k_ref[...]
    kg = kb * jnp.exp(-g).astype(BF16)
    kd = kb * jnp.exp(g_tot - g).astype(BF16)
    v = v_ref[...]
    order = list(reversed(range(n_sub))) if reverse else list(range(n_sub))
    e_tot = [jnp.exp(g_tot[i * c:i * c + 1, :]) for i in range(n_sub)]
    outs = []
    for h in range(HGRN_HEADS):
        cs = slice(h * HGRN_DK, (h + 1) * HGRN_DK)
        s = lax.dot_general(qg[:, cs], kg[:, cs], _NT, preferred_element_type=F32)
        p = jnp.where(keep, s, 0.0).astype(BF16)
        o_h = jnp.dot(p, v[:, cs], preferred_element_type=F32)
        st = s_scr[h]
        pieces = [None] * n_sub
        for i in order:
            rows = slice(i * c, (i + 1) * c)
            pieces[i] = o_h[rows, :] + lax.dot_general(
                qg[rows, cs], st.astype(BF16), _NT, preferred_element_type=F32)
            st = st * e_tot[i][:, cs] + lax.dot_general(
                v[rows, cs], kd[rows, cs], _TN, preferred_element_type=F32)
        s_scr[h] = st
        outs.append(jnp.concatenate(pieces, axis=0))
    return jnp.concatenate(outs, axis=-1)


def _hgrn_bwd_kernel(q_ref, v_ref, k_ref, lf_ref, ob_ref, s_scr):
    @pl.when(pl.program_id(1) == 0)
    def _():
        s_scr[...] = jnp.zeros_like(s_scr)

    ob_ref[...] = _hgrn_block(q_ref, k_ref, v_ref, lf_ref, s_scr, reverse=True).astype(BF16)


def _hgrn_fwd_kernel(q_ref, v_ref, k_ref, lf_ref, sg_ref, ob_ref, x_ref, nw_ref, wo_ref,
                     xy_ref, s_scr, o_scr):
    @pl.when(pl.program_id(1) == 0)
    def _():
        s_scr[...] = jnp.zeros_like(s_scr)

    o_scr[...] = (_hgrn_block(q_ref, k_ref, v_ref, lf_ref, s_scr, reverse=False)
                  + ob_ref[...].astype(F32))
    on = _rms_rows(o_scr[...], nw_ref[...])
    gated = (on * sg_ref[...].astype(F32)).astype(BF16)
    xn = (_from_tok_rows(x_ref[...])
          + jnp.dot(gated, wo_ref[...], preferred_element_type=F32))
    _store_xy(xy_ref, xn)


def _hgrn(a3, l3, xy3, norm_w, w_out):
    b, t, _ = a3.shape
    tb = HGRN_BLOCK
    nb = t // tb
    d = D_MODEL
    state = pltpu.VMEM((HGRN_HEADS, HGRN_DK, HGRN_DK), F32)

    ob = pl.pallas_call(
        _hgrn_bwd_kernel,
        grid=(b, nb),
        in_specs=[
            pl.BlockSpec((None, tb, d), lambda bi, i: (bi, nb - 1 - i, 0)),
            pl.BlockSpec((None, tb, d), lambda bi, i: (bi, nb - 1 - i, 1)),
            pl.BlockSpec((None, tb, d), lambda bi, i: (bi, nb - 1 - i, 3)),
            pl.BlockSpec((None, tb, d), lambda bi, i: (bi, nb - 1 - i, 1)),
        ],
        out_specs=pl.BlockSpec((None, tb, d), lambda bi, i: (bi, nb - 1 - i, 0)),
        out_shape=jax.ShapeDtypeStruct((b, t, d), BF16),
        scratch_shapes=[state],
        compiler_params=_cparams(("arbitrary", "arbitrary")),
        name="hgrn_bwd",
    )(a3, a3, a3, l3)

    return pl.pallas_call(
        _hgrn_fwd_kernel,
        grid=(b, nb),
        in_specs=[
            pl.BlockSpec((None, tb, d), lambda bi, i: (bi, i, 0)),
            pl.BlockSpec((None, tb, d), lambda bi, i: (bi, i, 1)),
            pl.BlockSpec((None, tb, d), lambda bi, i: (bi, i, 2)),
            pl.BlockSpec((None, tb, d), lambda bi, i: (bi, i, 0)),
            pl.BlockSpec((None, tb, d), lambda bi, i: (bi, i, 4)),
            pl.BlockSpec((None, tb, d), lambda bi, i: (bi, i, 0)),
            pl.BlockSpec((None, tb, X_ROWS, LANE), lambda bi, i: (bi, i, 1, 0)),
            pl.BlockSpec((1, d), lambda bi, i: (0, 0)),
            pl.BlockSpec((d, d), lambda bi, i: (0, 0)),
        ],
        out_specs=pl.BlockSpec((None, tb, TOK_ROWS, LANE), lambda bi, i: (bi, i, 0, 0)),
        out_shape=jax.ShapeDtypeStruct((b, t, TOK_ROWS, LANE), F32),
        scratch_shapes=[state, pltpu.VMEM((tb, d), F32)],
        compiler_params=_cparams(("arbitrary", "arbitrary")),
        name="hgrn_fwd",
    )(a3, a3, a3, l3, a3, ob, xy3, norm_w, w_out)


def _router_kernel(xy_ref, nw_ref, wr_ref, aff_ref):
    xn = _rms_rows(_from_tok_rows(xy_ref[...]), nw_ref[...])
    xh = xn.astype(BF16)
    xl = (xn - xh.astype(F32)).astype(BF16)
    wr = wr_ref[...]
    wh = wr.astype(BF16)
    wl = (wr - wh.astype(F32)).astype(BF16)
    logits = (lax.dot_general(wh, xh, _NT, preferred_element_type=F32)
              + lax.dot_general(wh, xl, _NT, preferred_element_type=F32)
              + lax.dot_general(wl, xh, _NT, preferred_element_type=F32))
    m = jnp.max(logits, axis=0, keepdims=True)
    e = jnp.exp(logits - m)
    aff_ref[...] = e / jnp.sum(e, axis=0, keepdims=True)


def _router(xy, nw, wr_t):
    n = xy.shape[0]
    return pl.pallas_call(
        _router_kernel,
        grid=(n // PROJ_TM,),
        in_specs=[
            pl.BlockSpec((PROJ_TM, X_ROWS, LANE), lambda i: (i, 0, 0)),
            pl.BlockSpec((1, D_MODEL), lambda i: (0, 0)),
            pl.BlockSpec((N_EXPERTS, D_MODEL), lambda i: (0, 0)),
        ],
        out_specs=pl.BlockSpec((N_EXPERTS, PROJ_TM), lambda i: (0, i)),
        out_shape=jax.ShapeDtypeStruct((N_EXPERTS, n), F32),
        compiler_params=_cparams(("arbitrary",)),
        name="router",
    )(xy, nw, wr_t)


def _ffn_kernel(idx_ref, idxf_ref, idxp_ref, idxn_ref, gate_ref, nw_ref, wg_ref, wu_ref, wd_ref,
                xy_in_ref, xy_ref, buf, stage, acc_scr, xn_scr, sem_g, sem_s, *,
                tiles_per_expert):
    del xy_in_ref
    tm = FFN_TM
    e = pl.program_id(0)
    i = pl.program_id(1)
    step = e * tiles_per_expert + i
    n_steps = pl.num_programs(0) * tiles_per_expert
    last_tile = i == tiles_per_expert - 1
    y_rows = pl.ds(X_ROWS, X_ROWS)

    def start_gather(ids, dst_slot, lo=0, hi=FFN_TM):
        for r in range(lo, hi):
            pltpu.make_async_copy(xy_ref.at[ids[0, r]], buf.at[dst_slot * tm + r],
                                  sem_g.at[dst_slot]).start()

    def wait_gather(dst_slot):
        pltpu.make_async_copy(xy_ref.at[pl.ds(0, tm)], buf.at[pl.ds(0, tm)],
                              sem_g.at[dst_slot]).wait()

    def start_scatter(ids, src_slot, lo=0, hi=FFN_TM):
        for r in range(lo, hi):
            pltpu.make_async_copy(stage.at[src_slot * tm + r], xy_ref.at[ids[0, r], y_rows],
                                  sem_s.at[0]).start()

    def wait_scatter():
        pltpu.make_async_copy(stage.at[pl.ds(0, tm)], xy_ref.at[pl.ds(0, tm), y_rows],
                              sem_s.at[0]).wait()

    @pl.when(step == 0)
    def _():
        start_gather(idx_ref, 0)

    def run(slot):
        cur = pl.ds(slot * tm, tm)
        prev = pl.ds((1 - slot) * tm, tm)
        wait_gather(slot)

        @pl.when(i == 0)
        def _():
            stage[prev] = buf[cur, X_ROWS:TOK_ROWS, :]

        xn_scr[...] = _rms_rows(_from_tok_rows(buf[cur, 0:X_ROWS, :]), nw_ref[...]).astype(BF16)

        n_f = EXPERT_FF // FFN_FW
        n_pts = 3 * n_f
        cuts = [(tm * k) // n_pts for k in range(n_pts + 1)]

        def issue(k):
            start_scatter(idxp_ref, 1 - slot, cuts[k], cuts[k + 1])
            start_gather(idxf_ref, 1 - slot, cuts[k], cuts[k + 1])

        eye = (lax.broadcasted_iota(jnp.int32, (LANE, LANE), 0)
               == lax.broadcasted_iota(jnp.int32, (LANE, LANE), 1)).astype(F32)
        gate_col = jnp.concatenate(
            [jnp.sum(eye * gate_ref[r:r + 1, :], axis=1, keepdims=True)
             for r in range(tm // LANE)], axis=0)
        for f in range(n_f):
            fs = slice(f * FFN_FW, (f + 1) * FFN_FW)
            issue(3 * f)
            g = jnp.dot(xn_scr[...], wg_ref[:, fs], preferred_element_type=F32)
            issue(3 * f + 1)
            u = jnp.dot(xn_scr[...], wu_ref[:, fs], preferred_element_type=F32)
            hcur = (g * _sigmoid(g) * u * gate_col).astype(BF16)
            issue(3 * f + 2)
            part = jnp.dot(hcur, wd_ref[fs, :], preferred_element_type=F32)
            if f == 0:
                acc_scr[...] = part
            elif f < n_f - 1:
                acc_scr[...] += part
            else:
                wait_scatter()
                stage[cur] = _to_tok_rows(acc_scr[...] + part) + buf[cur, X_ROWS:TOK_ROWS, :]

        @pl.when(last_tile)
        def _():
            wait_gather(1 - slot)
            start_scatter(idx_ref, slot)
            wait_scatter()

            @pl.when(step < n_steps - 1)
            def _():
                start_gather(idxn_ref, 1 - slot)

    for slot in (0, 1):
        pl.when(step % 2 == slot)(functools.partial(run, slot))


def _expert_ffn(xy, idx, gate, nw, wg, wu, wd, layer):
    cap = idx.shape[1]
    tm = FFN_TM
    nt = cap // tm
    assert nt >= 2 and EXPERT_FF // FFN_FW >= 3
    n_steps = N_EXPERTS * nt
    idx3 = idx.reshape(n_steps, 1, tm)
    gate3 = gate.reshape(n_steps, tm // LANE, LANE)
    smem_tile = functools.partial(pl.BlockSpec, (None, 1, tm), memory_space=pltpu.SMEM)
    return pl.pallas_call(
        functools.partial(_ffn_kernel, tiles_per_expert=nt),
        grid=(N_EXPERTS, nt),
        in_specs=[
            smem_tile(lambda e, i: (e * nt + i, 0, 0)),
            smem_tile(lambda e, i: (e * nt + jnp.minimum(i + 1, nt - 1), 0, 0)),
            smem_tile(lambda e, i: (e * nt + jnp.maximum(i - 1, 0), 0, 0)),
            smem_tile(lambda e, i: (jnp.minimum(e * nt + i + 1, n_steps - 1), 0, 0)),
            pl.BlockSpec((None, tm // LANE, LANE), lambda e, i: (e * nt + i, 0, 0)),
            pl.BlockSpec((1, D_MODEL), lambda e, i: (0, 0)),
            pl.BlockSpec((None, None, D_MODEL, EXPERT_FF), lambda e, i: (layer, e, 0, 0)),
            pl.BlockSpec((None, None, D_MODEL, EXPERT_FF), lambda e, i: (layer, e, 0, 0)),
            pl.BlockSpec((None, None, EXPERT_FF, D_MODEL), lambda e, i: (layer, e, 0, 0)),
            pl.BlockSpec(memory_space=pl.ANY),
        ],
        out_specs=pl.BlockSpec(memory_space=pl.ANY),
        out_shape=jax.ShapeDtypeStruct(xy.shape, F32),
        scratch_shapes=[
            pltpu.VMEM((2 * tm, TOK_ROWS, LANE), F32),
            pltpu.VMEM((2 * tm, X_ROWS, LANE), F32),
            pltpu.VMEM((tm, D_MODEL), F32),
            pltpu.VMEM((tm, D_MODEL), BF16),
            pltpu.SemaphoreType.DMA((2,)),
            pltpu.SemaphoreType.DMA((1,)),
        ],
        input_output_aliases={9: 0},
        compiler_params=_cparams(("arbitrary", "arbitrary")),
        name="expert_ffn",
    )(idx3, idx3, idx3, idx3, gate3, nw, wg, wu, wd, xy)


def _threshold_kernel(aff_ref, thr_ref, need_ref, *, k):
    bits = lax.bitcast_convert_type(aff_ref[...], I32)

    def body(b, t):
        cand = t | (jnp.int32(1) << (30 - b))
        cnt = jnp.sum((bits >= cand).astype(I32), axis=1, keepdims=True)
        return jnp.where(cnt >= k, cand, t)

    t = lax.fori_loop(0, 31, body, jnp.zeros((N_EXPERTS, 1), I32))
    n_gt = jnp.sum((bits > t).astype(I32), axis=1, keepdims=True)
    thr_ref[...] = jnp.broadcast_to(lax.bitcast_convert_type(t, F32), thr_ref.shape)
    need_ref[...] = jnp.broadcast_to(k - n_gt, need_ref.shape)


def _threshold(aff, k):
    return pl.pallas_call(
        functools.partial(_threshold_kernel, k=k),
        out_shape=[jax.ShapeDtypeStruct((N_EXPERTS, LANE), F32),
                   jax.ShapeDtypeStruct((N_EXPERTS, LANE), I32)],
        compiler_params=pltpu.CompilerParams(vmem_limit_bytes=VMEM_LIMIT_BYTES),
        name="topk_threshold",
    )(aff)


def _compact_sc(aff_flat, thr_flat, need_flat, n, cap):
    lanes = SC_LANES
    mesh = plsc.VectorSubcoreMesh(core_axis_name="c", subcore_axis_name="s", num_cores=1,
                                  num_subcores=N_EXPERTS)

    @functools.partial(
        pl.kernel,
        out_type=(jax.ShapeDtypeStruct((N_EXPERTS * cap,), I32),
                  jax.ShapeDtypeStruct((N_EXPERTS * cap,), F32)),
        mesh=mesh,
        scratch_types=[pltpu.VMEM((n,), F32), pltpu.VMEM((cap,), I32), pltpu.VMEM((cap,), F32),
                       pltpu.VMEM((lanes,), F32), pltpu.VMEM((lanes,), I32)],
        compiler_params=pltpu.CompilerParams(needs_layout_passes=False),
        name="topk_compact",
    )
    def compact(aff_hbm, thr_hbm, need_hbm, idx_hbm, gate_hbm, row, idx_v, gate_v, thr_v, need_v):
        e = lax.axis_index("s")
        pltpu.sync_copy(aff_hbm.at[pl.ds(e * n, n)], row)
        pltpu.sync_copy(thr_hbm.at[pl.ds(e * LANE, lanes)], thr_v)
        pltpu.sync_copy(need_hbm.at[pl.ds(e * LANE, lanes)], need_v)
        thr = thr_v[...]
        need = need_v[...]
        lane = lax.iota(I32, lanes)
        zero_i = jnp.zeros((lanes,), I32)

        def init(i, carry):
            idx_v[pl.ds(i * lanes, lanes)] = zero_i
            gate_v[pl.ds(i * lanes, lanes)] = jnp.zeros((lanes,), F32)
            return carry

        lax.fori_loop(0, cap // lanes, init, 0)

        def body(i, carry):
            off, eqs = carry
            v = row[pl.ds(i * lanes, lanes)]
            m_gt = v > thr
            m_eq = v == thr
            eq_rank = plsc.cumsum(m_eq.astype(I32)) + eqs
            take = m_gt | (m_eq & (eq_rank <= need))
            pos = off + plsc.cumsum(take.astype(I32)) - 1
            take = take & (pos < cap)
            plsc.store_scatter(idx_v, [pos], lane + i * lanes, mask=take)
            plsc.store_scatter(gate_v, [pos], v, mask=take)
            off = off + plsc.all_reduce_population_count(take)
            eqs = eqs + plsc.all_reduce_population_count(m_eq)
            return off, eqs

        lax.fori_loop(0, n // lanes, body, (zero_i, zero_i))
        pltpu.sync_copy(idx_v, idx_hbm.at[pl.ds(e * cap, cap)])
        pltpu.sync_copy(gate_v, gate_hbm.at[pl.ds(e * cap, cap)])

    return compact(aff_flat, thr_flat, need_flat)


def _expert_choice(aff, cap):
    n = aff.shape[1]
    thr, need = _threshold(aff, cap)
    idx, gate = _compact_sc(aff.reshape(-1), thr.reshape(-1), need.reshape(-1), n, cap)
    return gate.reshape(N_EXPERTS, cap), idx.reshape(N_EXPERTS, cap)


def _moe(xy, p, layer):
    n = xy.shape[0]
    cap = EC_CAPACITY_FACTOR * n // N_EXPERTS
    nw = p["norm_ffn"][layer]
    aff = _router(xy, nw, p["w_router_t"][layer])
    gate, idx = _expert_choice(aff, cap)
    return _expert_ffn(xy, idx, gate, nw, p["w_gate"], p["w_up"], p["w_down"], layer)


def _final_norm_kernel(xy_ref, nw_ref, o_ref):
    o_ref[...] = _rms_rows(_from_tok_rows(xy_ref[...]), nw_ref[...])


def _final_norm(xy, nw):
    n = xy.shape[0]
    return pl.pallas_call(
        _final_norm_kernel,
        grid=(n // PROJ_TM,),
        in_specs=[pl.BlockSpec((PROJ_TM, X_ROWS, LANE), lambda i: (i, 1, 0)),
                  pl.BlockSpec((1, D_MODEL), lambda i: (0, 0))],
        out_specs=pl.BlockSpec((PROJ_TM, D_MODEL), lambda i: (i, 0)),
        out_shape=jax.ShapeDtypeStruct((n, D_MODEL), F32),
        compiler_params=_cparams(("arbitrary",)),
        name="final_norm",
    )(xy, nw)


def _rope_tables(seq):
    half = RET_DK // 2
    inv = 1.0 / (ROPE_BASE ** (jnp.arange(half, dtype=F32) / half))
    ang = jnp.arange(seq, dtype=F32)[:, None] * inv[None, :]
    return jnp.cos(ang), jnp.sin(ang)


def _trunk(x, p):
    b, t, d = x.shape
    n = b * t
    cos, sin = _rope_tables(t)

    proj = _ret_inproj(x.reshape(n, d), 0, p["norm_mix"][0], p["ret_w_in"], cos, sin, t)
    xy = _retention(proj.reshape(b, t, -1), x, p["ret_decay"], p["ret_w_out"])
    xy = _moe(xy.reshape(n, TOK_ROWS, LANE), p, 0)

    a, lf = _hgrn_inproj(xy, p["norm_mix"][1], p["hgrn_w_in"], p["hgrn_lb"])
    xy = _hgrn(a.reshape(b, t, -1), lf.reshape(b, t, -1), xy.reshape(b, t, TOK_ROWS, LANE),
               p["hgrn_norm"], p["hgrn_w_out"])
    xy = _moe(xy.reshape(n, TOK_ROWS, LANE), p, 1)

    return _final_norm(xy, p["norm_final"]).reshape(b, t, d)


def kernel(x_prompt, x_sample, norm_mix_w, norm_ffn_w, norm_final_w, ret_w_in, ret_w_out,
           hgrn_w_in, hgrn_lb, hgrn_norm_w, hgrn_w_out, moe_w_router, moe_w_gate, moe_w_up,
           moe_w_down):
    depth = norm_mix_w.shape[0]
    assert depth == 2 and ret_w_in.shape[0] == 1 and hgrn_w_in.shape[0] == 1
    sm = jax.nn.softmax(hgrn_lb.astype(F32), axis=0)
    lower_bounds = jnp.cumsum(sm, axis=0) - sm[0:1]
    p = {
        "norm_mix": norm_mix_w.reshape(depth, 1, D_MODEL),
        "norm_ffn": norm_ffn_w.reshape(depth, 1, D_MODEL),
        "norm_final": norm_final_w.reshape(1, D_MODEL),
        "ret_w_in": ret_w_in[0].astype(BF16),
        "ret_w_out": ret_w_out[0].astype(BF16),
        "ret_decay": _ret_decay_matrix(),
        "hgrn_w_in": hgrn_w_in[0].astype(BF16),
        "hgrn_lb": lower_bounds[1].reshape(1, 2 * D_MODEL),
        "hgrn_norm": hgrn_norm_w[0].reshape(1, D_MODEL),
        "hgrn_w_out": hgrn_w_out[0].astype(BF16),
        "w_router_t": jnp.swapaxes(moe_w_router, 1, 2),
        "w_gate": moe_w_gate.astype(BF16),
        "w_up": moe_w_up.astype(BF16),
        "w_down": moe_w_down.astype(BF16),
    }
    return _trunk(x_prompt, p), _trunk(x_sample, p)
```

```python
import functools
import math

import jax
import jax.numpy as jnp
from jax import lax
from jax.experimental import pallas as pl
from jax.experimental.pallas import tpu as pltpu
from jax.experimental.pallas import tpu_sc as plsc

F32 = jnp.float32
I32 = jnp.int32
BF16 = jnp.bfloat16

D_MODEL = 1024
NORM_EPS = 1e-6
ROPE_BASE = 10000.0

RET_HEADS = 4
RET_DK = 256
RET_DV = 512
RET_QK_WIDTH = RET_HEADS * RET_DK
RET_V_WIDTH = RET_HEADS * RET_DV
RET_DECAY_FWD = 5.0
RET_DECAY_BWD = 5.5
RET_BLOCK = 512

HGRN_HEADS = 8
HGRN_DK = 128
HGRN_CHUNK = 64
HGRN_BLOCK = 256

N_EXPERTS = 16
EC_CAPACITY_FACTOR = 2
EXPERT_FF = 2 * D_MODEL

PROJ_TM = 512
PROJ_CW = 512
FFN_TM = 512
FFN_FW = 1024
LANE = 128
SC_LANES = 16
X_ROWS = D_MODEL // LANE
TOK_ROWS = 2 * X_ROWS

VMEM_LIMIT_BYTES = 52 * 1024 * 1024


def _cparams(sem):
    return pltpu.CompilerParams(dimension_semantics=sem, vmem_limit_bytes=VMEM_LIMIT_BYTES)


def _ret_log_gamma(offset):
    return [math.log1p(-(2.0 ** (-offset - h))) for h in range(RET_HEADS)]


def _rms_rows(x, w):
    ms = jnp.mean(x * x, axis=-1, keepdims=True)
    return x * lax.rsqrt(ms + NORM_EPS) * w


def _sigmoid(x):
    return 0.5 * jnp.tanh(0.5 * x) + 0.5


def _silu(x):
    h = 0.5 * x
    return h + h * jnp.tanh(h)


def _from_tok_rows(v):
    return v.reshape(v.shape[0], D_MODEL)


def _to_tok_rows(v):
    return v.reshape(v.shape[0], X_ROWS, LANE)


def _store_xy(xy_ref, val):
    r = _to_tok_rows(val)
    xy_ref[:, 0:X_ROWS, :] = r
    xy_ref[:, X_ROWS:TOK_ROWS, :] = r


def _ret_inproj_kernel(x_ref, nw_ref, w_ref, cos_ref, sin_ref, o_ref, xn_scr):
    xn_scr[...] = _rms_rows(x_ref[...], nw_ref[...]).astype(BF16)
    cos = cos_ref[...]
    sin = sin_ref[...]
    width = w_ref.shape[1]
    half = RET_DK // 2
    for j in range(width // PROJ_CW):
        c0 = j * PROJ_CW
        r = jnp.dot(xn_scr[...], w_ref[:, c0:c0 + PROJ_CW], preferred_element_type=F32)
        if c0 < 2 * RET_QK_WIDTH:
            scale = 1.0 if c0 < RET_QK_WIDTH else RET_DK ** -0.5
            parts = []
            for hh in range(PROJ_CW // RET_DK):
                x1 = r[:, hh * RET_DK:hh * RET_DK + half]
                x2 = r[:, hh * RET_DK + half:(hh + 1) * RET_DK]
                parts.append((x1 * cos - x2 * sin) * scale)
                parts.append((x1 * sin + x2 * cos) * scale)
            r = jnp.concatenate(parts, axis=-1)
        elif c0 >= 2 * RET_QK_WIDTH + RET_V_WIDTH:
            r = _silu(r)
        o_ref[:, c0:c0 + PROJ_CW] = r.astype(BF16)


def _ret_inproj(x2d, col, nw, w, cos, sin, seq):
    n = x2d.shape[0]
    width = w.shape[1]
    tpb = seq // PROJ_TM
    return pl.pallas_call(
        _ret_inproj_kernel,
        grid=(n // PROJ_TM,),
        in_specs=[
            pl.BlockSpec((PROJ_TM, D_MODEL), lambda i: (i, col)),
            pl.BlockSpec((1, D_MODEL), lambda i: (0, 0)),
            pl.BlockSpec((D_MODEL, width), lambda i: (0, 0)),
            pl.BlockSpec((PROJ_TM, RET_DK // 2), lambda i: (i % tpb, 0)),
            pl.BlockSpec((PROJ_TM, RET_DK // 2), lambda i: (i % tpb, 0)),
        ],
        out_specs=pl.BlockSpec((PROJ_TM, width), lambda i: (i, 0)),
        out_shape=jax.ShapeDtypeStruct((n, width), BF16),
        scratch_shapes=[pltpu.VMEM((PROJ_TM, D_MODEL), BF16)],
        compiler_params=_cparams(("arbitrary",)),
        name="ret_inproj",
    )(x2d, nw, w, cos, sin)


def _hgrn_inproj_kernel(xy_ref, nw_ref, w_ref, lb_ref, a_ref, l_ref, xn_scr):
    xn_scr[...] = _rms_rows(_from_tok_rows(xy_ref[...]), nw_ref[...]).astype(BF16)
    width = w_ref.shape[1]
    for j in range(width // PROJ_CW):
        c0 = j * PROJ_CW
        r = jnp.dot(xn_scr[...], w_ref[:, c0:c0 + PROJ_CW], preferred_element_type=F32)
        if c0 < D_MODEL:
            a_ref[:, c0:c0 + PROJ_CW] = _silu(r).astype(BF16)
        elif c0 < 2 * D_MODEL:
            a_ref[:, c0:c0 + PROJ_CW] = r.astype(BF16)
        elif c0 < 4 * D_MODEL:
            g0 = c0 - 2 * D_MODEL
            lb = lb_ref[:, g0:g0 + PROJ_CW]
            f = 0.5 * (1.0 + lb) + (0.5 * (1.0 - lb)) * jnp.tanh(0.5 * r)
            a_ref[:, c0:c0 + PROJ_CW] = (1.0 - f).astype(BF16)
            l_ref[:, g0:g0 + PROJ_CW] = jnp.log(f)
        else:
            a_ref[:, c0:c0 + PROJ_CW] = _sigmoid(r).astype(BF16)


def _hgrn_inproj(xy, nw, w, lb):
    n = xy.shape[0]
    width = w.shape[1]
    return pl.pallas_call(
        _hgrn_inproj_kernel,
        grid=(n // PROJ_TM,),
        in_specs=[
            pl.BlockSpec((PROJ_TM, X_ROWS, LANE), lambda i: (i, 1, 0)),
            pl.BlockSpec((1, D_MODEL), lambda i: (0, 0)),
            pl.BlockSpec((D_MODEL, width), lambda i: (0, 0)),
            pl.BlockSpec((1, 2 * D_MODEL), lambda i: (0, 0)),
        ],
        out_specs=[
            pl.BlockSpec((PROJ_TM, width), lambda i: (i, 0)),
            pl.BlockSpec((PROJ_TM, 2 * D_MODEL), lambda i: (i, 0)),
        ],
        out_shape=[
            jax.ShapeDtypeStruct((n, width), BF16),
            jax.ShapeDtypeStruct((n, 2 * D_MODEL), F32),
        ],
        scratch_shapes=[pltpu.VMEM((PROJ_TM, D_MODEL), BF16)],
        compiler_params=_cparams(("arbitrary",)),
        name="hgrn_inproj",
    )(xy, nw, w, lb)


_NT = (((1,), (1,)), ((), ()))
_TN = (((0,), (0,)), ((), ()))


def _ret_bwd_kernel(q_ref, k_ref, v_ref, ob_ref, r_scr, *, lg):
    @pl.when(pl.program_id(1) == 0)
    def _():
        r_scr[...] = jnp.zeros_like(r_scr)

    c = q_ref.shape[0]
    pos = lax.broadcasted_iota(jnp.int32, (c, 1), 0).astype(F32)
    for h in range(RET_HEADS):
        qh = q_ref[:, h * RET_DK:(h + 1) * RET_DK].astype(F32)
        kh = k_ref[:, h * RET_DK:(h + 1) * RET_DK].astype(F32)
        vh = v_ref[:, h * RET_DV:(h + 1) * RET_DV]
        qd = (qh * jnp.exp((c - pos) * lg[h])).astype(BF16)
        o = jnp.dot(qd, r_scr[h].astype(BF16), preferred_element_type=F32)
        ob_ref[:, h * RET_DV:(h + 1) * RET_DV] = o.astype(BF16)
        kd = (kh * jnp.exp(pos * lg[h])).astype(BF16)
        r_scr[h] = r_scr[h] * math.exp(c * lg[h]) + lax.dot_general(
            kd, vh, _TN, preferred_element_type=F32)


def _ret_fwd_kernel(q_ref, k_ref, v_ref, g_ref, ob_ref, x_ref, m_ref, wo_ref, xy_ref,
                    r_scr, *, lg):
    @pl.when(pl.program_id(1) == 0)
    def _():
        r_scr[...] = jnp.zeros_like(r_scr)

    c = q_ref.shape[0]
    pos = lax.broadcasted_iota(jnp.int32, (c, 1), 0).astype(F32)
    acc = x_ref[...]
    for h in range(RET_HEADS):
        qb = q_ref[:, h * RET_DK:(h + 1) * RET_DK]
        kb = k_ref[:, h * RET_DK:(h + 1) * RET_DK]
        vh = v_ref[:, h * RET_DV:(h + 1) * RET_DV]
        s = lax.dot_general(qb, kb, _NT, preferred_element_type=F32)
        sm = (s * m_ref[h]).astype(BF16)
        qd = (qb.astype(F32) * jnp.exp((pos + 1.0) * lg[h])).astype(BF16)
        o = (jnp.dot(sm, vh, preferred_element_type=F32)
             + jnp.dot(qd, r_scr[h].astype(BF16), preferred_element_type=F32)
             + ob_ref[:, h * RET_DV:(h + 1) * RET_DV].astype(F32))
        kd = (kb.astype(F32) * jnp.exp((c - 1.0 - pos) * lg[h])).astype(BF16)
        r_scr[h] = r_scr[h] * math.exp(c * lg[h]) + lax.dot_general(
            kd, vh, _TN, preferred_element_type=F32)
        ms = jnp.mean(o * o, axis=-1, keepdims=True)
        gated = (o * lax.rsqrt(ms + NORM_EPS)
                 * g_ref[:, h * RET_DV:(h + 1) * RET_DV].astype(F32)).astype(BF16)
        acc = acc + jnp.dot(gated, wo_ref[h * RET_DV:(h + 1) * RET_DV, :],
                            preferred_element_type=F32)
    _store_xy(xy_ref, acc)


def _retention(proj, x3, decay_mat, w_out):
    b, t, _ = proj.shape
    c = RET_BLOCK
    nc = t // c
    lgf = _ret_log_gamma(RET_DECAY_FWD)
    lgb = _ret_log_gamma(RET_DECAY_BWD)
    qk = RET_QK_WIDTH
    vw = RET_V_WIDTH
    state = pltpu.VMEM((RET_HEADS, RET_DK, RET_DV), F32)

    ob = pl.pallas_call(
        functools.partial(_ret_bwd_kernel, lg=lgb),
        grid=(b, nc),
        in_specs=[
            pl.BlockSpec((None, c, qk), lambda bi, i: (bi, nc - 1 - i, 0)),
            pl.BlockSpec((None, c, qk), lambda bi, i: (bi, nc - 1 - i, 1)),
            pl.BlockSpec((None, c, vw), lambda bi, i: (bi, nc - 1 - i, 1)),
        ],
        out_specs=pl.BlockSpec((None, c, vw), lambda bi, i: (bi, nc - 1 - i, 0)),
        out_shape=jax.ShapeDtypeStruct((b, t, vw), BF16),
        scratch_shapes=[state],
        compiler_params=_cparams(("arbitrary", "arbitrary")),
        name="ret_bwd",
    )(proj, proj, proj)

    return pl.pallas_call(
        functools.partial(_ret_fwd_kernel, lg=lgf),
        grid=(b, nc),
        in_specs=[
            pl.BlockSpec((None, c, qk), lambda bi, i: (bi, i, 0)),
            pl.BlockSpec((None, c, qk), lambda bi, i: (bi, i, 1)),
            pl.BlockSpec((None, c, vw), lambda bi, i: (bi, i, 1)),
            pl.BlockSpec((None, c, vw), lambda bi, i: (bi, i, 2)),
            pl.BlockSpec((None, c, vw), lambda bi, i: (bi, i, 0)),
            pl.BlockSpec((None, c, D_MODEL), lambda bi, i: (bi, i, 0)),
            pl.BlockSpec((RET_HEADS, c, c), lambda bi, i: (0, 0, 0)),
            pl.BlockSpec((vw, D_MODEL), lambda bi, i: (0, 0)),
        ],
        out_specs=pl.BlockSpec((None, c, TOK_ROWS, LANE), lambda bi, i: (bi, i, 0, 0)),
        out_shape=jax.ShapeDtypeStruct((b, t, TOK_ROWS, LANE), F32),
        scratch_shapes=[state],
        compiler_params=_cparams(("arbitrary", "arbitrary")),
        name="ret_fwd",
    )(proj, proj, proj, proj, ob, x3, decay_mat, w_out)


def _ret_decay_matrix():
    c = RET_BLOCK
    rel = (jnp.arange(c, dtype=F32)[:, None] - jnp.arange(c, dtype=F32)[None, :])[None]
    lgf = jnp.asarray(_ret_log_gamma(RET_DECAY_FWD), F32)[:, None, None]
    lgb = jnp.asarray(_ret_log_gamma(RET_DECAY_BWD), F32)[:, None, None]
    return jnp.where(rel >= 0, jnp.exp(jnp.maximum(rel, 0.0) * lgf),
                     jnp.exp(jnp.maximum(-rel, 0.0) * lgb))


def _tri_cumsum(tri, x):
    hi = x.astype(BF16)
    lo = (x - hi.astype(F32)).astype(BF16)
    return (jnp.dot(tri, hi, preferred_element_type=F32)
            + jnp.dot(tri, lo, preferred_element_type=F32))


def _hgrn_block(q_ref, k_ref, v_ref, lf_ref, s_scr, reverse):
    tb = HGRN_BLOCK
    c = HGRN_CHUNK
    n_sub = tb // c
    shift = c.bit_length() - 1
    row = lax.broadcasted_iota(jnp.int32, (tb, tb), 0)
    col = lax.broadcasted_iota(jnp.int32, (tb, tb), 1)
    same_chunk = lax.shift_right_logical(row, shift) == lax.shift_right_logical(col, shift)
    keep = same_chunk & ((col >= row) if reverse else (col <= row))
    g = _tri_cumsum(keep.astype(BF16), lf_ref[...])
    g_tot = jnp.concatenate(
        [jnp.broadcast_to(g[i * c:i * c + 1, :] if reverse else g[(i + 1) * c - 1:(i + 1) * c, :],
                          (c, g.shape[1])) for i in range(n_sub)], axis=0)
    qg = q_ref[...] * jnp.exp(g).astype(BF16)
    kb = k_ref[...]
    kg = kb * jnp.exp(-g).astype(BF16)
    kd = kb * jnp.exp(g_tot - g).astype(BF16)
    v = v_ref[...]
    order = list(reversed(range(n_sub))) if reverse else list(range(n_sub))
    e_tot = [jnp.exp(g_tot[i * c:i * c + 1, :]) for i in range(n_sub)]
    outs = []
    for h in range(HGRN_HEADS):
        cs = slice(h * HGRN_DK, (h + 1) * HGRN_DK)
        s = lax.dot_general(qg[:, cs], kg[:, cs], _NT, preferred_element_type=F32)
        p = jnp.where(keep, s, 0.0).astype(BF16)
        o_h = jnp.dot(p, v[:, cs], preferred_element_type=F32)
        st = s_scr[h]
        pieces = [None] * n_sub
        for i in order:
            rows = slice(i * c, (i + 1) * c)
            pieces[i] = o_h[rows, :] + lax.dot_general(
                qg[rows, cs], st.astype(BF16), _NT, preferred_element_type=F32)
            st = st * e_tot[i][:, cs] + lax.dot_general(
                v[rows, cs], kd[rows, cs], _TN, preferred_element_type=F32)
        s_scr[h] = st
        outs.append(jnp.concatenate(pieces, axis=0))
    return jnp.concatenate(outs, axis=-1)


def _hgrn_bwd_kernel(q_ref, v_ref, k_ref, lf_ref, ob_ref, s_scr):
    @pl.when(pl.program_id(1) == 0)
    def _():
        s_scr[...] = jnp.zeros_like(s_scr)

    ob_ref[...] = _hgrn_block(q_ref, k_ref, v_ref, lf_ref, s_scr, reverse=True).astype(BF16)


def _hgrn_fwd_kernel(q_ref, v_ref, k_ref, lf_ref, sg_ref, ob_ref, x_ref, nw_ref, wo_ref,
                     xy_ref, s_scr, o_scr):
    @pl.when(pl.program_id(1) == 0)
    def _():
        s_scr[...] = jnp.zeros_like(s_scr)

    o_scr[...] = (_hgrn_block(q_ref, k_ref, v_ref, lf_ref, s_scr, reverse=False)
                  + ob_ref[...].astype(F32))
    on = _rms_rows(o_scr[...], nw_ref[...])
    gated = (on * sg_ref[...].astype(F32)).astype(BF16)
    xn = (_from_tok_rows(x_ref[...])
          + jnp.dot(gated, wo_ref[...], preferred_element_type=F32))
    _store_xy(xy_ref, xn)


def _hgrn(a3, l3, xy3, norm_w, w_out):
    b, t, _ = a3.shape
    tb = HGRN_BLOCK
    nb = t // tb
    d = D_MODEL
    state = pltpu.VMEM((HGRN_HEADS, HGRN_DK, HGRN_DK), F32)

    ob = pl.pallas_call(
        _hgrn_bwd_kernel,
        grid=(b, nb),
        in_specs=[
            pl.BlockSpec((None, tb, d), lambda bi, i: (bi, nb - 1 - i, 0)),
            pl.BlockSpec((None, tb, d), lambda bi, i: (bi, nb - 1 - i, 1)),
            pl.BlockSpec((None, tb, d), lambda bi, i: (bi, nb - 1 - i, 3)),
            pl.BlockSpec((None, tb, d), lambda bi, i: (bi, nb - 1 - i, 1)),
        ],
        out_specs=pl.BlockSpec((None, tb, d), lambda bi, i: (bi, nb - 1 - i, 0)),
        out_shape=jax.ShapeDtypeStruct((b, t, d), BF16),
        scratch_shapes=[state],
        compiler_params=_cparams(("arbitrary", "arbitrary")),
        name="hgrn_bwd",
    )(a3, a3, a3, l3)

    return pl.pallas_call(
        _hgrn_fwd_kernel,
        grid=(b, nb),
        in_specs=[
            pl.BlockSpec((None, tb, d), lambda bi, i: (bi, i, 0)),
            pl.BlockSpec((None, tb, d), lambda bi, i: (bi, i, 1)),
            pl.BlockSpec((None, tb, d), lambda bi, i: (bi, i, 2)),
            pl.BlockSpec((None, tb, d), lambda bi, i: (bi, i, 0)),
            pl.BlockSpec((None, tb, d), lambda bi, i: (bi, i, 4)),
            pl.BlockSpec((None, tb, d), lambda bi, i: (bi, i, 0)),
            pl.BlockSpec((None, tb, X_ROWS, LANE), lambda bi, i: (bi, i, 1, 0)),
            pl.BlockSpec((1, d), lambda bi, i: (0, 0)),
            pl.BlockSpec((d, d), lambda bi, i: (0, 0)),
        ],
        out_specs=pl.BlockSpec((None, tb, TOK_ROWS, LANE), lambda bi, i: (bi, i, 0, 0)),
        out_shape=jax.ShapeDtypeStruct((b, t, TOK_ROWS, LANE), F32),
        scratch_shapes=[state, pltpu.VMEM((tb, d), F32)],
        compiler_params=_cparams(("arbitrary", "arbitrary")),
        name="hgrn_fwd",
    )(a3, a3, a3, l3, a3, ob, xy3, norm_w, w_out)


def _router_kernel(xy_ref, nw_ref, wr_ref, aff_ref):
    xn = _rms_rows(_from_tok_rows(xy_ref[...]), nw_ref[...])
    xh = xn.astype(BF16)
    xl = (xn - xh.astype(F32)).astype(BF16)
    wr = wr_ref[...]
    wh = wr.astype(BF16)
    wl = (wr - wh.astype(F32)).astype(BF16)
    logits = (lax.dot_general(wh, xh, _NT, preferred_element_type=F32)
              + lax.dot_general(wh, xl, _NT, preferred_element_type=F32)
              + lax.dot_general(wl, xh, _NT, preferred_element_type=F32))
    m = jnp.max(logits, axis=0, keepdims=True)
    e = jnp.exp(logits - m)
    aff_ref[...] = e / jnp.sum(e, axis=0, keepdims=True)


def _router(xy, nw, wr_t):
    n = xy.shape[0]
    return pl.pallas_call(
        _router_kernel,
        grid=(n // PROJ_TM,),
        in_specs=[
            pl.BlockSpec((PROJ_TM, X_ROWS, LANE), lambda i: (i, 0, 0)),
            pl.BlockSpec((1, D_MODEL), lambda i: (0, 0)),
            pl.BlockSpec((N_EXPERTS, D_MODEL), lambda i: (0, 0)),
        ],
        out_specs=pl.BlockSpec((N_EXPERTS, PROJ_TM), lambda i: (0, i)),
        out_shape=jax.ShapeDtypeStruct((N_EXPERTS, n), F32),
        compiler_params=_cparams(("arbitrary",)),
        name="router",
    )(xy, nw, wr_t)


def _ffn_kernel(idx_ref, idxf_ref, idxp_ref, idxn_ref, gate_ref, nw_ref, wg_ref, wu_ref, wd_ref,
                xy_in_ref, xy_ref, buf, stage, acc_scr, xn_scr, sem_g, sem_s, *,
                tiles_per_expert):
    del xy_in_ref
    tm = FFN_TM
    e = pl.program_id(0)
    i = pl.program_id(1)
    step = e * tiles_per_expert + i
    n_steps = pl.num_programs(0) * tiles_per_expert
    last_tile = i == tiles_per_expert - 1
    y_rows = pl.ds(X_ROWS, X_ROWS)

    def start_gather(ids, dst_slot, lo=0, hi=FFN_TM):
        for r in range(lo, hi):
            pltpu.make_async_copy(xy_ref.at[ids[0, r]], buf.at[dst_slot * tm + r],
                                  sem_g.at[dst_slot]).start()

    def wait_gather(dst_slot):
        pltpu.make_async_copy(xy_ref.at[pl.ds(0, tm)], buf.at[pl.ds(0, tm)],
                              sem_g.at[dst_slot]).wait()

    def start_scatter(ids, src_slot, lo=0, hi=FFN_TM):
        for r in range(lo, hi):
            pltpu.make_async_copy(stage.at[src_slot * tm + r], xy_ref.at[ids[0, r], y_rows],
                                  sem_s.at[0]).start()

    def wait_scatter():
        pltpu.make_async_copy(stage.at[pl.ds(0, tm)], xy_ref.at[pl.ds(0, tm), y_rows],
                              sem_s.at[0]).wait()

    @pl.when(step == 0)
    def _():
        start_gather(idx_ref, 0)

    def run(slot):
        cur = pl.ds(slot * tm, tm)
        prev = pl.ds((1 - slot) * tm, tm)
        wait_gather(slot)

        @pl.when(i == 0)
        def _():
            stage[prev] = buf[cur, X_ROWS:TOK_ROWS, :]

        xn_scr[...] = _rms_rows(_from_tok_rows(buf[cur, 0:X_ROWS, :]), nw_ref[...]).astype(BF16)

        n_f = EXPERT_FF // FFN_FW
        n_pts = 3 * n_f
        cuts = [(tm * k) // n_pts for k in range(n_pts + 1)]

        def issue(k):
            start_scatter(idxp_ref, 1 - slot, cuts[k], cuts[k + 1])
            start_gather(idxf_ref, 1 - slot, cuts[k], cuts[k + 1])

        eye = (lax.broadcasted_iota(jnp.int32, (LANE, LANE), 0)
               == lax.broadcasted_iota(jnp.int32, (LANE, LANE), 1)).astype(F32)
        gate_col = jnp.concatenate(
            [jnp.sum(eye * gate_ref[r:r + 1, :], axis=1, keepdims=True)
             for r in range(tm // LANE)], axis=0)
        for f in range(n_f):
            fs = slice(f * FFN_FW, (f + 1) * FFN_FW)
            issue(3 * f)
            g = jnp.dot(xn_scr[...], wg_ref[:, fs], preferred_element_type=F32)
            issue(3 * f + 1)
            u = jnp.dot(xn_scr[...], wu_ref[:, fs], preferred_element_type=F32)
            hcur = (g * _sigmoid(g) * u * gate_col).astype(BF16)
            issue(3 * f + 2)
            part = jnp.dot(hcur, wd_ref[fs, :], preferred_element_type=F32)
            if f == 0:
                acc_scr[...] = part
            elif f < n_f - 1:
                acc_scr[...] += part
            else:
                wait_scatter()
                stage[cur] = _to_tok_rows(acc_scr[...] + part) + buf[cur, X_ROWS:TOK_ROWS, :]

        @pl.when(last_tile)
        def _():
            wait_gather(1 - slot)
            start_scatter(idx_ref, slot)
            wait_scatter()

            @pl.when(step < n_steps - 1)
            def _():
                start_gather(idxn_ref, 1 - slot)

    for slot in (0, 1):
        pl.when(step % 2 == slot)(functools.partial(run, slot))


def _expert_ffn(xy, idx, gate, nw, wg, wu, wd, layer):
    cap = idx.shape[1]
    tm = FFN_TM
    nt = cap // tm
    assert nt >= 2 and EXPERT_FF // FFN_FW >= 2
    n_steps = N_EXPERTS * nt
    idx3 = idx.reshape(n_steps, 1, tm)
    gate3 = gate.reshape(n_steps, tm // LANE, LANE)
    smem_tile = functools.partial(pl.BlockSpec, (None, 1, tm), memory_space=pltpu.SMEM)
    return pl.pallas_call(
        functools.partial(_ffn_kernel, tiles_per_expert=nt),
        grid=(N_EXPERTS, nt),
        in_specs=[
            smem_tile(lambda e, i: (e * nt + i, 0, 0)),
            smem_tile(lambda e, i: (e * nt + jnp.minimum(i + 1, nt - 1), 0, 0)),
            smem_tile(lambda e, i: (e * nt + jnp.maximum(i - 1, 0), 0, 0)),
            smem_tile(lambda e, i: (jnp.minimum(e * nt + i + 1, n_steps - 1), 0, 0)),
            pl.BlockSpec((None, tm // LANE, LANE), lambda e, i: (e * nt + i, 0, 0)),
            pl.BlockSpec((1, D_MODEL), lambda e, i: (0, 0)),
            pl.BlockSpec((None, None, D_MODEL, EXPERT_FF), lambda e, i: (layer, e, 0, 0)),
            pl.BlockSpec((None, None, D_MODEL, EXPERT_FF), lambda e, i: (layer, e, 0, 0)),
            pl.BlockSpec((None, None, EXPERT_FF, D_MODEL), lambda e, i: (layer, e, 0, 0)),
            pl.BlockSpec(memory_space=pl.ANY),
        ],
        out_specs=pl.BlockSpec(memory_space=pl.ANY),
        out_shape=jax.ShapeDtypeStruct(xy.shape, F32),
        scratch_shapes=[
            pltpu.VMEM((2 * tm, TOK_ROWS, LANE), F32),
            pltpu.VMEM((2 * tm, X_ROWS, LANE), F32),
            pltpu.VMEM((tm, D_MODEL), F32),
            pltpu.VMEM((tm, D_MODEL), BF16),
            pltpu.SemaphoreType.DMA((2,)),
            pltpu.SemaphoreType.DMA((1,)),
        ],
        input_output_aliases={9: 0},
        compiler_params=_cparams(("arbitrary", "arbitrary")),
        name="expert_ffn",
    )(idx3, idx3, idx3, idx3, gate3, nw, wg, wu, wd, xy)


def _threshold_kernel(aff_ref, thr_ref, need_ref, *, k):
    bits = lax.bitcast_convert_type(aff_ref[...], I32)

    def body(b, t):
        cand = t | (jnp.int32(1) << (30 - b))
        cnt = jnp.sum((bits >= cand).astype(I32), axis=1, keepdims=True)
        return jnp.where(cnt >= k, cand, t)

    t = lax.fori_loop(0, 31, body, jnp.zeros((N_EXPERTS, 1), I32))
    n_gt = jnp.sum((bits > t).astype(I32), axis=1, keepdims=True)
    thr_ref[...] = jnp.broadcast_to(lax.bitcast_convert_type(t, F32), thr_ref.shape)
    need_ref[...] = jnp.broadcast_to(k - n_gt, need_ref.shape)


def _threshold(aff, k):
    return pl.pallas_call(
        functools.partial(_threshold_kernel, k=k),
        out_shape=[jax.ShapeDtypeStruct((N_EXPERTS, LANE), F32),
                   jax.ShapeDtypeStruct((N_EXPERTS, LANE), I32)],
        compiler_params=pltpu.CompilerParams(vmem_limit_bytes=VMEM_LIMIT_BYTES),
        name="topk_threshold",
    )(aff)


def _compact_sc(aff_flat, thr_flat, need_flat, n, cap):
    lanes = SC_LANES
    mesh = plsc.VectorSubcoreMesh(core_axis_name="c", subcore_axis_name="s", num_cores=1,
                                  num_subcores=N_EXPERTS)

    @functools.partial(
        pl.kernel,
        out_type=(jax.ShapeDtypeStruct((N_EXPERTS * cap,), I32),
                  jax.ShapeDtypeStruct((N_EXPERTS * cap,), F32)),
        mesh=mesh,
        scratch_types=[pltpu.VMEM((n,), F32), pltpu.VMEM((cap,), I32), pltpu.VMEM((cap,), F32),
                       pltpu.VMEM((lanes,), F32), pltpu.VMEM((lanes,), I32)],
        compiler_params=pltpu.CompilerParams(needs_layout_passes=False),
        name="topk_compact",
    )
    def compact(aff_hbm, thr_hbm, need_hbm, idx_hbm, gate_hbm, row, idx_v, gate_v, thr_v, need_v):
        e = lax.axis_index("s")
        pltpu.sync_copy(aff_hbm.at[pl.ds(e * n, n)], row)
        pltpu.sync_copy(thr_hbm.at[pl.ds(e * LANE, lanes)], thr_v)
        pltpu.sync_copy(need_hbm.at[pl.ds(e * LANE, lanes)], need_v)
        thr = thr_v[...]
        need = need_v[...]
        lane = lax.iota(I32, lanes)
        zero_i = jnp.zeros((lanes,), I32)

        def init(i, carry):
            idx_v[pl.ds(i * lanes, lanes)] = zero_i
            gate_v[pl.ds(i * lanes, lanes)] = jnp.zeros((lanes,), F32)
            return carry

        lax.fori_loop(0, cap // lanes, init, 0)

        def body(i, carry):
            off, eqs = carry
            v = row[pl.ds(i * lanes, lanes)]
            m_gt = v > thr
            m_eq = v == thr
            eq_rank = plsc.cumsum(m_eq.astype(I32)) + eqs
            take = m_gt | (m_eq & (eq_rank <= need))
            pos = off + plsc.cumsum(take.astype(I32)) - 1
            take = take & (pos < cap)
            plsc.store_scatter(idx_v, [pos], lane + i * lanes, mask=take)
            plsc.store_scatter(gate_v, [pos], v, mask=take)
            off = off + plsc.all_reduce_population_count(take)
            eqs = eqs + plsc.all_reduce_population_count(m_eq)
            return off, eqs

        lax.fori_loop(0, n // lanes, body, (zero_i, zero_i))
        pltpu.sync_copy(idx_v, idx_hbm.at[pl.ds(e * cap, cap)])
        pltpu.sync_copy(gate_v, gate_hbm.at[pl.ds(e * cap, cap)])

    return compact(aff_flat, thr_flat, need_flat)


def _expert_choice(aff, cap):
    n = aff.shape[1]
    thr, need = _threshold(aff, cap)
    idx, gate = _compact_sc(aff.reshape(-1), thr.reshape(-1), need.reshape(-1), n, cap)
    return gate.reshape(N_EXPERTS, cap), idx.reshape(N_EXPERTS, cap)


def _moe(xy, p, layer):
    n = xy.shape[0]
    cap = EC_CAPACITY_FACTOR * n // N_EXPERTS
    nw = p["norm_ffn"][layer]
    aff = _router(xy, nw, p["w_router_t"][layer])
    gate, idx = _expert_choice(aff, cap)
    return _expert_ffn(xy, idx, gate, nw, p["w_gate"], p["w_up"], p["w_down"], layer)


def _final_norm_kernel(xy_ref, nw_ref, o_ref):
    o_ref[...] = _rms_rows(_from_tok_rows(xy_ref[...]), nw_ref[...])


def _final_norm(xy, nw):
    n = xy.shape[0]
    return pl.pallas_call(
        _final_norm_kernel,
        grid=(n // PROJ_TM,),
        in_specs=[pl.BlockSpec((PROJ_TM, X_ROWS, LANE), lambda i: (i, 1, 0)),
                  pl.BlockSpec((1, D_MODEL), lambda i: (0, 0))],
        out_specs=pl.BlockSpec((PROJ_TM, D_MODEL), lambda i: (i, 0)),
        out_shape=jax.ShapeDtypeStruct((n, D_MODEL), F32),
        compiler_params=_cparams(("arbitrary",)),
        name="final_norm",
    )(xy, nw)


def _rope_tables(seq):
    half = RET_DK // 2
    inv = 1.0 / (ROPE_BASE ** (jnp.arange(half, dtype=F32) / half))
    ang = jnp.arange(seq, dtype=F32)[:, None] * inv[None, :]
    return jnp.cos(ang), jnp.sin(ang)


def _trunk(x, p):
    b, t, d = x.shape
    n = b * t
    cos, sin = _rope_tables(t)

    proj = _ret_inproj(x.reshape(n, d), 0, p["norm_mix"][0], p["ret_w_in"], cos, sin, t)
    xy = _retention(proj.reshape(b, t, -1), x, p["ret_decay"], p["ret_w_out"])
    xy = _moe(xy.reshape(n, TOK_ROWS, LANE), p, 0)

    a, lf = _hgrn_inproj(xy, p["norm_mix"][1], p["hgrn_w_in"], p["hgrn_lb"])
    xy = _hgrn(a.reshape(b, t, -1), lf.reshape(b, t, -1), xy.reshape(b, t, TOK_ROWS, LANE),
               p["hgrn_norm"], p["hgrn_w_out"])
    xy = _moe(xy.reshape(n, TOK_ROWS, LANE), p, 1)

    return _final_norm(xy, p["norm_final"]).reshape(b, t, d)


def kernel(x_prompt, x_sample, norm_mix_w, norm_ffn_w, norm_final_w, ret_w_in, ret_w_out,
           hgrn_w_in, hgrn_lb, hgrn_norm_w, hgrn_w_out, moe_w_router, moe_w_gate, moe_w_up,
           moe_w_down):
    depth = norm_mix_w.shape[0]
    assert depth == 2 and ret_w_in.shape[0] == 1 and hgrn_w_in.shape[0] == 1
    sm = jax.nn.softmax(hgrn_lb.astype(F32), axis=0)
    lower_bounds = jnp.cumsum(sm, axis=0) - sm[0:1]
    p = {
        "norm_mix": norm_mix_w.reshape(depth, 1, D_MODEL),
        "norm_ffn": norm_ffn_w.reshape(depth, 1, D_MODEL),
        "norm_final": norm_final_w.reshape(1, D_MODEL),
        "ret_w_in": ret_w_in[0].astype(BF16),
        "ret_w_out": ret_w_out[0].astype(BF16),
        "ret_decay": _ret_decay_matrix(),
        "hgrn_w_in": hgrn_w_in[0].astype(BF16),
        "hgrn_lb": lower_bounds[1].reshape(1, 2 * D_MODEL),
        "hgrn_norm": hgrn_norm_w[0].reshape(1, D_MODEL),
        "hgrn_w_out": hgrn_w_out[0].astype(BF16),
        "w_router_t": jnp.swapaxes(moe_w_router, 1, 2),
        "w_gate": moe_w_gate.astype(BF16),
        "w_up": moe_w_up.astype(BF16),
        "w_down": moe_w_down.astype(BF16),
    }
    return _trunk(x_prompt, p), _trunk(x_sample, p)
```
